```python
import jax
import jax.numpy as jnp
from jax import lax
import numpy as np

D_MODEL = 1024
BATCH = 2
SEQ = 8192
DEPTH = 4
DEC_BATCH = 32
DEC_SEQ = 64
PAST_LEN = 2048

CHUNK = 64
N_META = 16
N_MIXERS = 2
N_CONV_LAYERS = (DEPTH + 1) // 2
N_ATTN_LAYERS = DEPTH // 2
D_FF = 2816
CONV_WIDTH = 31
HEAD_DIM = 64
N_HEADS = D_MODEL // HEAD_DIM
N_KV_HEADS = max(1, N_HEADS // 8)
GROUP = N_HEADS // N_KV_HEADS
WINDOW = 128
WINDOW_CHUNKS = WINDOW // CHUNK
EPS = 1e-6

kernel_name = 'hybrid_conformer_conv_swa_sink_stream_step'


def rms_norm(x, g):
    xf = x.astype(jnp.float32)
    y = xf * lax.rsqrt(jnp.mean(xf * xf, axis=-1, keepdims=True) + EPS)
    return (y * g.astype(jnp.float32)).astype(x.dtype)


def layer_norm(x, g, b):
    xf = x.astype(jnp.float32)
    mu = jnp.mean(xf, axis=-1, keepdims=True)
    var = jnp.mean(jnp.square(xf - mu), axis=-1, keepdims=True)
    y = (xf - mu) * lax.rsqrt(var + EPS) * g.astype(jnp.float32) + b.astype(jnp.float32)
    return y.astype(x.dtype)


def swiglu_half(x, g, w_gate, w_up, w_down):
    h = rms_norm(x, g)
    return x + 0.5 * ((jax.nn.silu(h @ w_gate) * (h @ w_up)) @ w_down)


def conv_module(x, left, g, w_pw1, b_pw1, w_dw, b_dw, ln_g, ln_b, w_pw2, b_pw2):
    h = rms_norm(x, g)
    a = h @ w_pw1 + b_pw1
    u = a[..., :D_MODEL] * jax.nn.sigmoid(a[..., D_MODEL:])
    u_pad = jnp.concatenate([left.astype(u.dtype), u], axis=1)
    c = lax.conv_general_dilated(
        u_pad, w_dw[:, None, :].astype(u.dtype), window_strides=(1,), padding='VALID',
        dimension_numbers=('NWC', 'WIO', 'NWC'), feature_group_count=D_MODEL) + b_dw
    c = jax.nn.silu(layer_norm(c, ln_g, ln_b))
    return x + (c @ w_pw2 + b_pw2), u_pad[:, -(CONV_WIDTH - 1):]


def qkv_proj(x, g, w_qkv, q_gain, k_gain):
    b, t = x.shape[0], x.shape[1]
    qkv = rms_norm(x, g) @ w_qkv
    nq = N_HEADS * HEAD_DIM
    nk = N_KV_HEADS * HEAD_DIM
    q = qkv[..., :nq].reshape(b, t, N_HEADS, HEAD_DIM)
    k = qkv[..., nq:nq + nk].reshape(b, t, N_KV_HEADS, HEAD_DIM)
    v = qkv[..., nq + nk:].reshape(b, t, N_KV_HEADS, HEAD_DIM)
    return rms_norm(q, q_gain), rms_norm(k, k_gain), v


def sink_attention(q, k, v, valid, sinks):
    b, nb, lq = q.shape[0], q.shape[1], q.shape[2]
    qg = q.reshape(b, nb, lq, N_KV_HEADS, GROUP, HEAD_DIM)
    s = jnp.einsum('bnqkgd,bnskd->bnkgqs', qg, k).astype(jnp.float32) * (HEAD_DIM ** -0.5)
    s = jnp.where(valid[None, :, None, None, None, :], s, -jnp.inf)
    sink = sinks.astype(jnp.float32).reshape(1, 1, N_KV_HEADS, GROUP, 1, 1)
    m = jnp.maximum(jnp.max(s, axis=-1, keepdims=True), sink)
    p = jnp.exp(s - m)
    p = p / (jnp.sum(p, axis=-1, keepdims=True) + jnp.exp(sink - m))
    o = jnp.einsum('bnkgqs,bnskd->bnqkgd', p.astype(v.dtype), v)
    return o.reshape(b, nb, lq, N_HEADS * HEAD_DIM)


def attn_prompt(x, g, w_qkv, q_gain, k_gain, sinks, w_o):
    b, t = x.shape[0], x.shape[1]
    nc = (t - N_META) // CHUNK
    q, k, v = qkv_proj(x, g, w_qkv, q_gain, k_gain)
    qm, km, vm = q[:, :N_META], k[:, :N_META], v[:, :N_META]
    qr = q[:, N_META:].reshape(b, nc, CHUNK, N_HEADS, HEAD_DIM)
    kr = k[:, N_META:].reshape(b, nc, CHUNK, N_KV_HEADS, HEAD_DIM)
    vr = v[:, N_META:].reshape(b, nc, CHUNK, N_KV_HEADS, HEAD_DIM)

    def band(tc, tm):
        shifted = [jnp.pad(tc, ((0, 0), (j, 0), (0, 0), (0, 0), (0, 0)))[:, :nc]
                   for j in range(WINDOW_CHUNKS, 0, -1)]
        meta = jnp.broadcast_to(tm[:, None], (b, nc) + tm.shape[1:])
        return jnp.concatenate([meta] + shifted + [tc], axis=2)

    cidx = jnp.arange(nc)[:, None]
    valid = jnp.concatenate(
        [jnp.ones((nc, N_META), bool)]
        + [jnp.broadcast_to(cidx >= j, (nc, CHUNK)) for j in range(WINDOW_CHUNKS, 0, -1)]
        + [jnp.ones((nc, CHUNK), bool)], axis=1)
    o_real = sink_attention(qr, band(kr, km), band(vr, vm), valid, sinks).reshape(b, nc * CHUNK, N_HEADS * HEAD_DIM)
    o_meta = sink_attention(qm[:, None], km[:, None], vm[:, None], jnp.ones((1, N_META), bool), sinks).reshape(b, N_META, N_HEADS * HEAD_DIM)
    o = jnp.concatenate([o_meta, o_real], axis=1)
    return x + o @ w_o, k[:, -WINDOW:], v[:, -WINDOW:], km, vm


def attn_sample(x, c_k, c_v, m_k, m_v, g, w_qkv, q_gain, k_gain, sinks, w_o):
    b, t = x.shape[0], x.shape[1]
    q, k, v = qkv_proj(x, g, w_qkv, q_gain, k_gain)
    kk = jnp.concatenate([m_k.astype(k.dtype), c_k.astype(k.dtype), k], axis=1)
    vv = jnp.concatenate([m_v.astype(v.dtype), c_v.astype(v.dtype), v], axis=1)
    valid = jnp.ones((1, kk.shape[1]), bool)
    o = sink_attention(q[:, None], kk[:, None], vv[:, None], valid, sinks).reshape(b, t, N_HEADS * HEAD_DIM)
    new_k = jnp.concatenate([c_k.astype(k.dtype), k], axis=1)[:, -WINDOW:]
    new_v = jnp.concatenate([c_v.astype(v.dtype), v], axis=1)[:, -WINDOW:]
    return x + o @ w_o, new_k, new_v


def setup_inputs(seed: int = 0) -> dict:
    key = jax.random.key(seed)
    ks = iter(jax.random.split(key, 48))

    def nrm(shape, scale):
        return jax.random.normal(next(ks), shape, jnp.float32) * scale

    def gain(shape):
        return 1.0 + nrm(shape, 0.02)

    na, ncv = N_ATTN_LAYERS, N_CONV_LAYERS
    qkv_w = (N_HEADS + 2 * N_KV_HEADS) * HEAD_DIM
    return {
        'x_prompt': nrm((BATCH, SEQ, D_MODEL), 1.0),
        'x_sample': nrm((DEC_BATCH, DEC_SEQ, D_MODEL), 1.0),
        'cache_swa_k': nrm((na, DEC_BATCH, WINDOW, N_KV_HEADS, HEAD_DIM), 1.0),
        'cache_swa_v': nrm((na, DEC_BATCH, WINDOW, N_KV_HEADS, HEAD_DIM), 1.0),
        'cache_meta_k': nrm((na, DEC_BATCH, N_META, N_KV_HEADS, HEAD_DIM), 1.0),
        'cache_meta_v': nrm((na, DEC_BATCH, N_META, N_KV_HEADS, HEAD_DIM), 1.0),
        'state_conv': nrm((ncv, DEC_BATCH, CONV_WIDTH - 1, D_MODEL), 0.5),
        'meta_tokens': nrm((N_META, D_MODEL), 1.0),
        'ffn1_norm': gain((DEPTH, D_MODEL)),
        'ffn1_w_gate': nrm((DEPTH, D_MODEL, D_FF), D_MODEL ** -0.5),
        'ffn1_w_up': nrm((DEPTH, D_MODEL, D_FF), D_MODEL ** -0.5),
        'ffn1_w_down': nrm((DEPTH, D_FF, D_MODEL), D_FF ** -0.5),
        'ffn2_norm': gain((DEPTH, D_MODEL)),
        'ffn2_w_gate': nrm((DEPTH, D_MODEL, D_FF), D_MODEL ** -0.5),
        'ffn2_w_up': nrm((DEPTH, D_MODEL, D_FF), D_MODEL ** -0.5),
        'ffn2_w_down': nrm((DEPTH, D_FF, D_MODEL), D_FF ** -0.5),
        'conv_norm': gain((ncv, D_MODEL)),
        'conv_w_pw1': nrm((ncv, D_MODEL, 2 * D_MODEL), D_MODEL ** -0.5),
        'conv_b_pw1': nrm((ncv, 2 * D_MODEL), 0.02),
        'conv_w_dw': nrm((ncv, CONV_WIDTH, D_MODEL), CONV_WIDTH ** -0.5),
        'conv_b_dw': nrm((ncv, D_MODEL), 0.02),
        'conv_ln_g': gain((ncv, D_MODEL)),
        'conv_ln_b': nrm((ncv, D_MODEL), 0.02),
        'conv_w_pw2': nrm((ncv, D_MODEL, D_MODEL), D_MODEL ** -0.5),
        'conv_b_pw2': nrm((ncv, D_MODEL), 0.02),
        'attn_norm': gain((na, D_MODEL)),
        'attn_w_qkv': nrm((na, D_MODEL, qkv_w), D_MODEL ** -0.5),
        'attn_q_gain': gain((na, HEAD_DIM)),
        'attn_k_gain': gain((na, HEAD_DIM)),
        'attn_sinks': nrm((na, N_HEADS), 0.5),
        'attn_w_o': nrm((na, N_HEADS * HEAD_DIM, D_MODEL), (N_HEADS * HEAD_DIM) ** -0.5),
        'final_norm': gain((D_MODEL,)),
    }


def reference(x_prompt, x_sample, cache_swa_k, cache_swa_v, cache_meta_k, cache_meta_v, state_conv,
              meta_tokens, ffn1_norm, ffn1_w_gate, ffn1_w_up, ffn1_w_down,
              ffn2_norm, ffn2_w_gate, ffn2_w_up, ffn2_w_down,
              conv_norm, conv_w_pw1, conv_b_pw1, conv_w_dw, conv_b_dw, conv_ln_g, conv_ln_b,
              conv_w_pw2, conv_b_pw2,
              attn_norm, attn_w_qkv, attn_q_gain, attn_k_gain, attn_sinks, attn_w_o,
              final_norm):
    bp = x_prompt.shape[0]
    meta = jnp.broadcast_to(meta_tokens[None].astype(x_prompt.dtype), (bp, N_META, D_MODEL))
    xp = jnp.concatenate([meta, x_prompt], axis=1)
    xs = x_sample
    conv_p, conv_s = [], []
    swk_p, swv_p, mk_p, mv_p, swk_s, swv_s = [], [], [], [], [], []
    for i in range(DEPTH):
        f1 = (ffn1_norm[i], ffn1_w_gate[i], ffn1_w_up[i], ffn1_w_down[i])
        xp = swiglu_half(xp, *f1)
        xs = swiglu_half(xs, *f1)
        j = i // N_MIXERS
        if i % N_MIXERS == 0:
            cw = (conv_norm[j], conv_w_pw1[j], conv_b_pw1[j], conv_w_dw[j], conv_b_dw[j],
                  conv_ln_g[j], conv_ln_b[j], conv_w_pw2[j], conv_b_pw2[j])
            xp, st_p = conv_module(xp, jnp.zeros((bp, CONV_WIDTH - 1, D_MODEL), xp.dtype), *cw)
            xs, st_s = conv_module(xs, state_conv[j], *cw)
            conv_p.append(st_p)
            conv_s.append(st_s)
        else:
            aw = (attn_norm[j], attn_w_qkv[j], attn_q_gain[j], attn_k_gain[j], attn_sinks[j], attn_w_o[j])
            xp, kp, vp, km, vm = attn_prompt(xp, *aw)
            xs, ksn, vsn = attn_sample(xs, cache_swa_k[j], cache_swa_v[j], cache_meta_k[j], cache_meta_v[j], *aw)
            swk_p.append(kp)
            swv_p.append(vp)
            mk_p.append(km)
            mv_p.append(vm)
            swk_s.append(ksn)
            swv_s.append(vsn)
        f2 = (ffn2_norm[i], ffn2_w_gate[i], ffn2_w_up[i], ffn2_w_down[i])
        xp = swiglu_half(xp, *f2)
        xs = swiglu_half(xs, *f2)
    y_prompt = rms_norm(xp[:, N_META:], final_norm)
    y_sample = rms_norm(xs, final_norm)
    return (y_prompt, y_sample,
            jnp.stack(swk_p), jnp.stack(swv_p), jnp.stack(mk_p), jnp.stack(mv_p), jnp.stack(conv_p),
            jnp.stack(swk_s), jnp.stack(swv_s), jnp.stack(conv_s))
```

```python
import functools

import numpy as np
import jax
import jax.numpy as jnp
from jax import lax
from jax.experimental import pallas as pl
from jax.experimental.pallas import tpu as pltpu

D_MODEL = 1024
D_FF = 2816
DEPTH = 4
N_META = 16
CHUNK = 64
CONV_WIDTH = 31
HEAD_DIM = 64
N_HEADS = 16
N_KV_HEADS = 2
GROUP = 8
WINDOW = 128
EPS = 1e-6

F32 = jnp.float32
BF16 = jnp.bfloat16

KV_WIDTH = N_KV_HEADS * HEAD_DIM
REP_WIDTH = GROUP * HEAD_DIM
HALO = 32
CONV_BASE = HALO - (CONV_WIDTH - 1)

ROW_TILE = 512
FLAT_TILE = 512
DW_ROWS = 32
FF_SPLITS = ((0, 1024), (1024, 2048), (2048, 2816))

VMEM_LIMIT = 56 * 1024 * 1024


def _params():
    return pltpu.CompilerParams(dimension_semantics=("arbitrary",), vmem_limit_bytes=VMEM_LIMIT)


def _const_spec(shape):
    nd = len(shape)
    return pl.BlockSpec(shape, lambda i: (0,) * nd, pipeline_mode=pl.Buffered(1))


def _rows_spec(rows, width, offset_blocks=0):
    return pl.BlockSpec((rows, width), lambda i: (i + offset_blocks, 0))


def _rms(x, g):
    ms = jnp.mean(x * x, axis=-1, keepdims=True)
    return (x * lax.rsqrt(ms + EPS)) * g


def _dot(a, b):
    return jnp.dot(a, b, preferred_element_type=F32)


def _ffn_kernel(x_ref, g_ref, wg_ref, wu_ref, wd_ref, fg_ref, o_ref, *, final):
    x = x_ref[...]
    h = _rms(x, g_ref[...]).astype(BF16)
    acc = None
    for c0, c1 in FF_SPLITS:
        gate = _dot(h, wg_ref[:, c0:c1])
        up = _dot(h, wu_ref[:, c0:c1])
        a = (gate * jax.nn.sigmoid(gate) * up).astype(BF16)
        part = _dot(a, wd_ref[c0:c1, :])
        acc = part if acc is None else acc + part
    y = x + 0.5 * acc
    if final:
        y = _rms(y, fg_ref[...])
    o_ref[...] = y


def _ffn(x, g, wg, wu, wd, fg, *, final):
    n = x.shape[0]
    return pl.pallas_call(
        functools.partial(_ffn_kernel, final=final),
        grid=(n // FLAT_TILE,),
        in_specs=[_rows_spec(FLAT_TILE, D_MODEL), _const_spec((1, D_MODEL)),
                  _const_spec((D_MODEL, D_FF)), _const_spec((D_MODEL, D_FF)),
                  _const_spec((D_FF, D_MODEL)), _const_spec((1, D_MODEL))],
        out_specs=_rows_spec(FLAT_TILE, D_MODEL),
        out_shape=jax.ShapeDtypeStruct((n, D_MODEL), F32),
        compiler_params=_params(),
        name="ffn_final" if final else "ffn",
    )(x, g, wg, wu, wd, fg)


def _glu_kernel(x_ref, g_ref, w_ref, b_ref, u_ref):
    h = _rms(x_ref[...], g_ref[...]).astype(BF16)
    a = _dot(h, w_ref[...]) + b_ref[...]
    u_ref[...] = a[:, :D_MODEL] * jax.nn.sigmoid(a[:, D_MODEL:])


def _glu(x, g, w, b):
    n = x.shape[0]
    return pl.pallas_call(
        _glu_kernel,
        grid=(n // FLAT_TILE,),
        in_specs=[_rows_spec(FLAT_TILE, D_MODEL), _const_spec((1, D_MODEL)),
                  _const_spec((D_MODEL, 2 * D_MODEL)), _const_spec((1, 2 * D_MODEL))],
        out_specs=_rows_spec(FLAT_TILE, D_MODEL),
        out_shape=jax.ShapeDtypeStruct((n, D_MODEL), F32),
        compiler_params=_params(),
        name="glu",
    )(x, g, w, b)


def _dwconv(ubuf, cbuf, src, dst, rows, wdw_ref):
    for r0 in range(0, rows, DW_ROWS):
        nr = min(DW_ROWS, rows - r0)
        acc = None
        for k in range(CONV_WIDTH):
            term = ubuf[src + r0 + k:src + r0 + k + nr, :] * wdw_ref[k:k + 1, :]
            acc = term if acc is None else acc + term
        cbuf[dst + r0:dst + r0 + nr, :] = acc


def _conv_tail(c, x, bdw, lng, lnb, w2, b2):
    c = c + bdw
    mu = jnp.mean(c, axis=-1, keepdims=True)
    cc = c - mu
    var = jnp.mean(cc * cc, axis=-1, keepdims=True)
    y = cc * lax.rsqrt(var + EPS) * lng + lnb
    y = (y * jax.nn.sigmoid(y)).astype(BF16)
    return x + (_dot(y, w2) + b2)


def _conv_prompt_kernel(u_ref, halo_ref, umeta_ref, x_ref, wdw_ref, bdw_ref, lng_ref, lnb_ref,
                        w2_ref, b2_ref, o_ref, ubuf, cbuf, *, tiles_per_stream):
    first = pl.program_id(0) % tiles_per_stream == 0

    @pl.when(first)
    def _():
        ubuf[0:HALO - N_META, :] = jnp.zeros((HALO - N_META, D_MODEL), F32)
        ubuf[HALO - N_META:HALO, :] = umeta_ref[...]

    @pl.when(jnp.logical_not(first))
    def _():
        ubuf[0:HALO, :] = halo_ref[...]

    ubuf[HALO:HALO + ROW_TILE, :] = u_ref[...]
    _dwconv(ubuf, cbuf, CONV_BASE, 0, ROW_TILE, wdw_ref)
    o_ref[...] = _conv_tail(cbuf[...], x_ref[...], bdw_ref[...], lng_ref[...], lnb_ref[...],
                            w2_ref[...], b2_ref[...])


def _conv_sample_kernel(u_ref, st_ref, x_ref, wdw_ref, bdw_ref, lng_ref, lnb_ref,
                        w2_ref, b2_ref, o_ref, ubuf, cbuf, *, streams):
    span = HALO + CHUNK
    for s in range(streams):
        ubuf[s * span + CONV_BASE:s * span + HALO, :] = st_ref[s]
        ubuf[s * span + HALO:(s + 1) * span, :] = u_ref[s * CHUNK:(s + 1) * CHUNK, :]
        _dwconv(ubuf, cbuf, s * span + CONV_BASE, s * CHUNK, CHUNK, wdw_ref)
    o_ref[...] = _conv_tail(cbuf[...], x_ref[...], bdw_ref[...], lng_ref[...], lnb_ref[...],
                            w2_ref[...], b2_ref[...])


def _conv_meta_kernel(u_ref, x_ref, wdw_ref, bdw_ref, lng_ref, lnb_ref, w2_ref, b2_ref,
                      o_ref, ubuf, cbuf):
    ubuf[0:HALO, :] = jnp.zeros((HALO, D_MODEL), F32)
    ubuf[HALO:HALO + N_META, :] = u_ref[...]
    _dwconv(ubuf, cbuf, CONV_BASE, 0, N_META, wdw_ref)
    o_ref[...] = _conv_tail(cbuf[...], x_ref[...], bdw_ref[...], lng_ref[...], lnb_ref[...],
                            w2_ref[...], b2_ref[...])


def _conv_weight_specs():
    return [_const_spec((CONV_WIDTH, D_MODEL)), _const_spec((1, D_MODEL)), _const_spec((1, D_MODEL)),
            _const_spec((1, D_MODEL)), _const_spec((D_MODEL, D_MODEL)), _const_spec((1, D_MODEL))]


def _conv_layer(x, u, state, cw, lay):
    wdw, bdw, lng, lnb, w2, b2 = cw
    n = x.shape[0]
    out_shape = jax.ShapeDtypeStruct((n, D_MODEL), F32)
    tiles_per_stream = lay["seq"] // ROW_TILE
    n_prompt_tiles = lay["n_prompt"] // ROW_TILE
    meta_blk = lay["meta_off"] // N_META

    x = pl.pallas_call(
        functools.partial(_conv_prompt_kernel, tiles_per_stream=tiles_per_stream),
        grid=(n_prompt_tiles,),
        in_specs=[_rows_spec(ROW_TILE, D_MODEL),
                  pl.BlockSpec((HALO, D_MODEL),
                               lambda i: (jnp.maximum(i * (ROW_TILE // HALO) - 1, 0), 0)),
                  pl.BlockSpec((N_META, D_MODEL), lambda i: (meta_blk, 0)),
                  _rows_spec(ROW_TILE, D_MODEL)] + _conv_weight_specs(),
        out_specs=_rows_spec(ROW_TILE, D_MODEL),
        out_shape=out_shape,
        scratch_shapes=[pltpu.VMEM((HALO + ROW_TILE, D_MODEL), F32),
                        pltpu.VMEM((ROW_TILE, D_MODEL), F32)],
        input_output_aliases={3: 0},
        compiler_params=_params(),
        name="conv_prompt",
    )(u, u, u, x, wdw, bdw, lng, lnb, w2, b2)

    streams = ROW_TILE // CHUNK
    x = pl.pallas_call(
        functools.partial(_conv_sample_kernel, streams=streams),
        grid=(lay["n_sample"] // ROW_TILE,),
        in_specs=[_rows_spec(ROW_TILE, D_MODEL, n_prompt_tiles),
                  pl.BlockSpec((streams, CONV_WIDTH - 1, D_MODEL), lambda i: (i, 0, 0)),
                  _rows_spec(ROW_TILE, D_MODEL, n_prompt_tiles)] + _conv_weight_specs(),
        out_specs=_rows_spec(ROW_TILE, D_MODEL, n_prompt_tiles),
        out_shape=out_shape,
        scratch_shapes=[pltpu.VMEM((streams * (HALO + CHUNK), D_MODEL), F32),
                        pltpu.VMEM((ROW_TILE, D_MODEL), F32)],
        input_output_aliases={2: 0},
        compiler_params=_params(),
        name="conv_sample",
    )(u, state, x, wdw, bdw, lng, lnb, w2, b2)

    meta_spec = pl.BlockSpec((N_META, D_MODEL), lambda i: (meta_blk, 0))
    x = pl.pallas_call(
        _conv_meta_kernel,
        grid=(1,),
        in_specs=[meta_spec, meta_spec] + _conv_weight_specs(),
        out_specs=meta_spec,
        out_shape=out_shape,
        scratch_shapes=[pltpu.VMEM((HALO + N_META, D_MODEL), F32),
                        pltpu.VMEM((N_META, D_MODEL), F32)],
        input_output_aliases={1: 0},
        compiler_params=_params(),
        name="conv_meta",
    )(u, x, wdw, bdw, lng, lnb, w2, b2)
    return x


def _split_dot(a, b):
    hi = a.astype(BF16)
    lo = (a - hi.astype(F32)).astype(BF16)
    return _dot(hi, b) + _dot(lo, b)


def _qkv_kernel(x_ref, g_ref, w_ref, bq_ref, bk_ref, e_ref, gq_ref, gk_ref,
                q_ref, krep_ref, vrep_ref, k_ref, v_ref):
    h = _rms(x_ref[...], g_ref[...]).astype(BF16)
    qkv = _dot(h, w_ref[...])
    nq = N_HEADS * HEAD_DIM
    q = qkv[:, :nq]
    k = qkv[:, nq:nq + KV_WIDTH]
    v = qkv[:, nq + KV_WIDTH:]
    q = (q * lax.rsqrt(_split_dot(q * q, bq_ref[...]) + EPS)) * gq_ref[...]
    k = (k * lax.rsqrt(_split_dot(k * k, bk_ref[...]) + EPS)) * gk_ref[...]
    q_ref[...] = (q * (HEAD_DIM ** -0.5)).astype(BF16)
    k_ref[...] = k
    v_ref[...] = v
    krep_ref[...] = _dot(k.astype(BF16), e_ref[...]).astype(BF16)
    vrep_ref[...] = _dot(v.astype(BF16), e_ref[...]).astype(BF16)


def _qkv(x, g, w, bq, bk, e, gq, gk):
    n = x.shape[0]
    qkv_w = (N_HEADS + 2 * N_KV_HEADS) * HEAD_DIM
    wide = N_KV_HEADS * REP_WIDTH
    return pl.pallas_call(
        _qkv_kernel,
        grid=(n // FLAT_TILE,),
        in_specs=[_rows_spec(FLAT_TILE, D_MODEL), _const_spec((1, D_MODEL)),
                  _const_spec((D_MODEL, qkv_w)), _const_spec((D_MODEL, D_MODEL)),
                  _const_spec((KV_WIDTH, KV_WIDTH)), _const_spec((KV_WIDTH, wide)),
                  _const_spec((1, D_MODEL)), _const_spec((1, KV_WIDTH))],
        out_specs=[_rows_spec(FLAT_TILE, D_MODEL), _rows_spec(FLAT_TILE, wide),
                   _rows_spec(FLAT_TILE, wide), _rows_spec(FLAT_TILE, KV_WIDTH),
                   _rows_spec(FLAT_TILE, KV_WIDTH)],
        out_shape=[jax.ShapeDtypeStruct((n, D_MODEL), BF16), jax.ShapeDtypeStruct((n, wide), BF16),
                   jax.ShapeDtypeStruct((n, wide), BF16), jax.ShapeDtypeStruct((n, KV_WIDTH), F32),
                   jax.ShapeDtypeStruct((n, KV_WIDTH), F32)],
        compiler_params=_params(),
        name="qkv",
    )(x, g, w, bq, bk, e, gq, gk)


def _attend(qg, kx, vx, sink_col, bias):
    nq = qg.shape[0]
    rows = GROUP * nq
    row_head = lax.broadcasted_iota(jnp.int32, (rows, REP_WIDTH), 0) // nq
    lane_head = lax.broadcasted_iota(jnp.int32, (rows, REP_WIDTH), 1) // HEAD_DIM
    qs = jnp.where(row_head == lane_head, jnp.concatenate([qg] * GROUP, axis=0),
                   jnp.zeros((), qg.dtype))
    s = lax.dot_general(qs, kx, (((1,), (1,)), ((), ())), preferred_element_type=F32)
    if bias is not None:
        s = s + bias
    m = jnp.maximum(jnp.max(s, axis=-1, keepdims=True), sink_col)
    p = jnp.exp(s - m)
    den = jnp.sum(p, axis=-1, keepdims=True) + jnp.exp(sink_col - m)
    pn = (p * (1.0 / den)).astype(BF16)
    ow = _dot(pn, vx)
    out_head = lax.broadcasted_iota(jnp.int32, (nq, REP_WIDTH), 1) // HEAD_DIM
    o = None
    for g in range(GROUP):
        t = jnp.where(out_head == g, ow[g * nq:(g + 1) * nq, :], 0.0)
        o = t if o is None else o + t
    return o


def _attn_prompt_kernel(q_ref, k_ref, khalo_ref, kmeta_ref, v_ref, vhalo_ref, vmeta_ref,
                        x_ref, sink_ref, wo_ref, o_ref, obuf, *, tiles_per_stream):
    chunks = ROW_TILE // CHUNK
    first_chunk = (pl.program_id(0) % tiles_per_stream) * chunks
    key_pos = lax.broadcasted_iota(jnp.int32, (1, WINDOW + CHUNK + N_META), 1)

    def band(main_ref, halo_ref, meta_ref, c, lanes):
        parts = []
        for back in (2, 1, 0):
            j = c - back
            if j < 0:
                parts.append(halo_ref[(2 + j) * CHUNK:(3 + j) * CHUNK, lanes])
            else:
                parts.append(main_ref[j * CHUNK:(j + 1) * CHUNK, lanes])
        parts.append(meta_ref[:, lanes])
        return jnp.concatenate(parts, axis=0)

    for c in range(chunks):
        missing = jnp.clip(2 - (first_chunk + c), 0, 2) * CHUNK
        bias = jnp.where(key_pos < missing, -jnp.inf, 0.0).astype(F32)
        for kh in range(N_KV_HEADS):
            lanes = slice(kh * REP_WIDTH, (kh + 1) * REP_WIDTH)
            o = _attend(q_ref[c * CHUNK:(c + 1) * CHUNK, lanes],
                        band(k_ref, khalo_ref, kmeta_ref, c, lanes),
                        band(v_ref, vhalo_ref, vmeta_ref, c, lanes),
                        sink_ref[kh * GROUP * CHUNK:(kh + 1) * GROUP * CHUNK, :], bias)
            obuf[c * CHUNK:(c + 1) * CHUNK, lanes] = o.astype(BF16)
    o_ref[...] = x_ref[...] + _dot(obuf[...], wo_ref[...])


def _attn_sample_kernel(q_ref, k_ref, ck_ref, mk_ref, v_ref, cv_ref, mv_ref,
                        x_ref, sink_ref, wo_ref, o_ref, obuf, *, streams):
    for s in range(streams):
        rows = slice(s * CHUNK, (s + 1) * CHUNK)
        for kh in range(N_KV_HEADS):
            lanes = slice(kh * REP_WIDTH, (kh + 1) * REP_WIDTH)
            kx = jnp.concatenate([ck_ref[s, :, lanes], k_ref[rows, lanes], mk_ref[s, :, lanes]], axis=0)
            vx = jnp.concatenate([cv_ref[s, :, lanes], v_ref[rows, lanes], mv_ref[s, :, lanes]], axis=0)
            o = _attend(q_ref[rows, lanes], kx, vx,
                        sink_ref[kh * GROUP * CHUNK:(kh + 1) * GROUP * CHUNK, :], None)
            obuf[rows, lanes] = o.astype(BF16)
    o_ref[...] = x_ref[...] + _dot(obuf[...], wo_ref[...])


def _attn_meta_kernel(q_ref, k_ref, v_ref, x_ref, sink_ref, wo_ref, o_ref, obuf):
    for kh in range(N_KV_HEADS):
        lanes = slice(kh * REP_WIDTH, (kh + 1) * REP_WIDTH)
        o = _attend(q_ref[:, lanes], k_ref[:, lanes], v_ref[:, lanes],
                    sink_ref[kh * GROUP * N_META:(kh + 1) * GROUP * N_META, :], None)
        obuf[:, lanes] = o.astype(BF16)
    o_ref[...] = x_ref[...] + _dot(obuf[...], wo_ref[...])


def _attn_layer(x, q, krep, vrep, ck, cv, mk, mv, sinks, wo, lay):
    n = x.shape[0]
    out_shape = jax.ShapeDtypeStruct((n, D_MODEL), F32)
    wide = N_KV_HEADS * REP_WIDTH
    tiles_per_stream = lay["seq"] // ROW_TILE
    n_prompt_tiles = lay["n_prompt"] // ROW_TILE
    meta_blk = lay["meta_off"] // N_META
    sink_chunk = jnp.repeat(sinks.astype(F32), CHUNK)[:, None]
    sink_meta = jnp.repeat(sinks.astype(F32), N_META)[:, None]

    main = _rows_spec(ROW_TILE, wide)
    halo = pl.BlockSpec((WINDOW, wide), lambda i: (jnp.maximum(i * (ROW_TILE // WINDOW) - 1, 0), 0))
    meta = pl.BlockSpec((N_META, wide), lambda i: (meta_blk, 0))
    x = pl.pallas_call(
        functools.partial(_attn_prompt_kernel, tiles_per_stream=tiles_per_stream),
        grid=(n_prompt_tiles,),
        in_specs=[main, main, halo, meta, main, halo, meta, _rows_spec(ROW_TILE, D_MODEL),
                  _const_spec((N_HEADS * CHUNK, 1)), _const_spec((D_MODEL, D_MODEL))],
        out_specs=_rows_spec(ROW_TILE, D_MODEL),
        out_shape=out_shape,
        scratch_shapes=[pltpu.VMEM((ROW_TILE, D_MODEL), BF16)],
        input_output_aliases={7: 0},
        compiler_params=_params(),
        name="attn_prompt",
    )(q, krep, krep, krep, vrep, vrep, vrep, x, sink_chunk, wo)

    streams = ROW_TILE // CHUNK
    smain = _rows_spec(ROW_TILE, wide, n_prompt_tiles)
    cache = pl.BlockSpec((streams, WINDOW, wide), lambda i: (i, 0, 0))
    mcache = pl.BlockSpec((streams, N_META, wide), lambda i: (i, 0, 0))
    x = pl.pallas_call(
        functools.partial(_attn_sample_kernel, streams=streams),
        grid=(lay["n_sample"] // ROW_TILE,),
        in_specs=[smain, smain, cache, mcache, smain, cache, mcache,
                  _rows_spec(ROW_TILE, D_MODEL, n_prompt_tiles),
                  _const_spec((N_HEADS * CHUNK, 1)), _const_spec((D_MODEL, D_MODEL))],
        out_specs=_rows_spec(ROW_TILE, D_MODEL, n_prompt_tiles),
        out_shape=out_shape,
        scratch_shapes=[pltpu.VMEM((ROW_TILE, D_MODEL), BF16)],
        input_output_aliases={7: 0},
        compiler_params=_params(),
        name="attn_sample",
    )(q, krep, ck, mk, vrep, cv, mv, x, sink_chunk, wo)

    mrow = pl.BlockSpec((N_META, D_MODEL), lambda i: (meta_blk, 0))
    x = pl.pallas_call(
        _attn_meta_kernel,
        grid=(1,),
        in_specs=[meta, meta, meta, mrow, _const_spec((N_HEADS * N_META, 1)),
                  _const_spec((D_MODEL, D_MODEL))],
        out_specs=mrow,
        out_shape=out_shape,
        scratch_shapes=[pltpu.VMEM((N_META, D_MODEL), BF16)],
        input_output_aliases={3: 0},
        compiler_params=_params(),
        name="attn_meta",
    )(q, krep, vrep, x, sink_meta, wo)
    return x


def _head_mean_matrix(width):
    idx = np.arange(width) // HEAD_DIM
    return jnp.asarray((idx[:, None] == idx[None, :]).astype(np.float32) / HEAD_DIM, dtype=BF16)


def _repeat_matrix():
    src = np.arange(KV_WIDTH)
    dst = np.arange(N_KV_HEADS * REP_WIDTH)
    same_head = (src[:, None] // HEAD_DIM) == (dst[None, :] // REP_WIDTH)
    same_dim = (src[:, None] % HEAD_DIM) == (dst[None, :] % HEAD_DIM)
    return jnp.asarray((same_head & same_dim).astype(np.float32), dtype=BF16)


def _repeat_heads(c):
    c = c.astype(BF16)
    c = jnp.broadcast_to(c[..., :, None, :], c.shape[:-1] + (GROUP, HEAD_DIM))
    return c.reshape(c.shape[:-4] + (c.shape[-4], N_KV_HEADS * REP_WIDTH))


def kernel(x_prompt, x_sample, cache_swa_k, cache_swa_v, cache_meta_k, cache_meta_v, state_conv, meta_tokens, ffn1_norm, ffn1_w_gate, ffn1_w_up, ffn1_w_down, ffn2_norm, ffn2_w_gate, ffn2_w_up, ffn2_w_down, conv_norm, conv_w_pw1, conv_b_pw1, conv_w_dw, conv_b_dw, conv_ln_g, conv_ln_b, conv_w_pw2, conv_b_pw2, attn_norm, attn_w_qkv, attn_q_gain, attn_k_gain, attn_sinks, attn_w_o, final_norm):
    batch, seq, _ = x_prompt.shape
    dec_batch, dec_seq, _ = x_sample.shape
    assert seq % ROW_TILE == 0 and dec_seq == CHUNK and (dec_batch * dec_seq) % ROW_TILE == 0
    n_prompt = batch * seq
    n_sample = dec_batch * dec_seq
    meta_off = n_prompt + n_sample
    n_flat = meta_off + ROW_TILE
    assert n_flat % FLAT_TILE == 0 and meta_off % ROW_TILE == 0
    lay = dict(seq=seq, n_prompt=n_prompt, n_sample=n_sample, meta_off=meta_off)

    x = jnp.concatenate([x_prompt.reshape(n_prompt, D_MODEL), x_sample.reshape(n_sample, D_MODEL),
                         meta_tokens.astype(F32), jnp.zeros((ROW_TILE - N_META, D_MODEL), F32)], axis=0)

    row = lambda a: a.reshape(1, -1).astype(F32)
    bq = _head_mean_matrix(D_MODEL)
    bk = _head_mean_matrix(KV_WIDTH)
    rep = _repeat_matrix()

    conv_p, conv_s = [], []
    swk_p, swv_p, mk_p, mv_p, swk_s, swv_s = [], [], [], [], [], []
    for i in range(DEPTH):
        x = _ffn(x, row(ffn1_norm[i]), ffn1_w_gate[i].astype(BF16), ffn1_w_up[i].astype(BF16),
                 ffn1_w_down[i].astype(BF16), row(final_norm), final=False)
        j = i // 2
        if i % 2 == 0:
            u = _glu(x, row(conv_norm[j]), conv_w_pw1[j].astype(BF16), row(conv_b_pw1[j]))
            cw = (conv_w_dw[j].astype(F32), row(conv_b_dw[j]), row(conv_ln_g[j]), row(conv_ln_b[j]),
                  conv_w_pw2[j].astype(BF16), row(conv_b_pw2[j]))
            x = _conv_layer(x, u, state_conv[j].astype(F32), cw, lay)
            conv_p.append(u[:n_prompt].reshape(batch, seq, D_MODEL)[:, seq - (CONV_WIDTH - 1):])
            conv_s.append(u[n_prompt:meta_off].reshape(dec_batch, dec_seq, D_MODEL)[:, dec_seq - (CONV_WIDTH - 1):])
        else:
            q, krep, vrep, k, v = _qkv(x, row(attn_norm[j]), attn_w_qkv[j].astype(BF16), bq, bk, rep,
                                       row(jnp.tile(attn_q_gain[j], N_HEADS)),
                                       row(jnp.tile(attn_k_gain[j], N_KV_HEADS)))
            x = _attn_layer(x, q, krep, vrep,
                            _repeat_heads(cache_swa_k[j]), _repeat_heads(cache_swa_v[j]),
                            _repeat_heads(cache_meta_k[j]), _repeat_heads(cache_meta_v[j]),
                            attn_sinks[j], attn_w_o[j].astype(BF16), lay)
            heads = lambda a, b, t: a.reshape(b, t, N_KV_HEADS, HEAD_DIM)
            kp, vp = heads(k[:n_prompt], batch, seq), heads(v[:n_prompt], batch, seq)
            swk_p.append(kp[:, seq - WINDOW:])
            swv_p.append(vp[:, seq - WINDOW:])
            km = heads(k[meta_off:meta_off + N_META], 1, N_META)
            vm = heads(v[meta_off:meta_off + N_META], 1, N_META)
            mk_p.append(jnp.broadcast_to(km, (batch,) + km.shape[1:]))
            mv_p.append(jnp.broadcast_to(vm, (batch,) + vm.shape[1:]))
            ks, vs = heads(k[n_prompt:meta_off], dec_batch, dec_seq), heads(v[n_prompt:meta_off], dec_batch, dec_seq)
            swk_s.append(jnp.concatenate([cache_swa_k[j].astype(F32), ks], axis=1)[:, dec_seq:])
            swv_s.append(jnp.concatenate([cache_swa_v[j].astype(F32), vs], axis=1)[:, dec_seq:])
        x = _ffn(x, row(ffn2_norm[i]), ffn2_w_gate[i].astype(BF16), ffn2_w_up[i].astype(BF16),
                 ffn2_w_down[i].astype(BF16), row(final_norm), final=(i == DEPTH - 1))

    y_prompt = x[:n_prompt].reshape(batch, seq, D_MODEL)
    y_sample = x[n_prompt:meta_off].reshape(dec_batch, dec_seq, D_MODEL)
    return (y_prompt, y_sample,
            jnp.stack(swk_p), jnp.stack(swv_p), jnp.stack(mk_p), jnp.stack(mv_p), jnp.stack(conv_p),
            jnp.stack(swk_s), jnp.stack(swv_s), jnp.stack(conv_s))
```

```python
import functools

import numpy as np
import jax
import jax.numpy as jnp
from jax import lax
from jax.experimental import pallas as pl
from jax.experimental.pallas import tpu as pltpu

D_MODEL = 1024
D_FF = 2816
DEPTH = 4
N_META = 16
CHUNK = 64
CONV_WIDTH = 31
HEAD_DIM = 64
N_HEADS = 16
N_KV_HEADS = 2
GROUP = 8
WINDOW = 128
EPS = 1e-6

F32 = jnp.float32
BF16 = jnp.bfloat16

LANES = 128
SLABS = D_MODEL // LANES
KV_WIDTH = N_KV_HEADS * HEAD_DIM
PAIR_WIDTH = 2 * HEAD_DIM
REP_WIDTH = GROUP * HEAD_DIM
N_KEYS = WINDOW + CHUNK + N_META
HALO = 32
CONV_BASE = HALO - (CONV_WIDTH - 1)
DW_GROUP = 16
DW_UNROLL = 4

ROW_TILE = 512
FF_SPLITS = ((0, 1024), (1024, 2048), (2048, 2816))

VMEM_LIMIT = 56 * 1024 * 1024


def _params():
    return pltpu.CompilerParams(dimension_semantics=("arbitrary",), vmem_limit_bytes=VMEM_LIMIT)


def _const_spec(shape):
    nd = len(shape)
    return pl.BlockSpec(shape, lambda i: (0,) * nd, pipeline_mode=pl.Buffered(1))


def _layer_spec(shape, layer):
    return pl.BlockSpec((None,) + shape, lambda i: (layer, 0, 0), pipeline_mode=pl.Buffered(1))


def _rows_spec(rows, width, offset_blocks=0):
    return pl.BlockSpec((rows, width), lambda i: (i + offset_blocks, 0))


def _rms(x, g):
    ms = jnp.mean(x * x, axis=-1, keepdims=True)
    return (x * lax.rsqrt(ms + EPS)) * g


def _dot(a, b):
    return jnp.dot(a, b, preferred_element_type=F32)


def _swiglu_half(x, g_ref, wg_ref, wu_ref, wd_ref):
    h = _rms(x, g_ref[...]).astype(BF16)
    acc = None
    for c0, c1 in FF_SPLITS:
        gate = _dot(h, wg_ref[:, c0:c1])
        up = _dot(h, wu_ref[:, c0:c1])
        a = (gate * jax.nn.sigmoid(gate) * up).astype(BF16)
        part = _dot(a, wd_ref[c0:c1, :])
        acc = part if acc is None else acc + part
    return x + 0.5 * acc


def _ffn_kernel(x_ref, g_ref, wg_ref, wu_ref, wd_ref, o_ref):
    o_ref[...] = _swiglu_half(x_ref[...], g_ref, wg_ref, wu_ref, wd_ref)


def _ffn_first_kernel(xp_ref, xs_ref, meta_ref, g_ref, wg_ref, wu_ref, wd_ref, o_ref, xbuf,
                      *, prompt_tiles, sample_tiles):
    i = pl.program_id(0)

    @pl.when(i < prompt_tiles)
    def _():
        xbuf[...] = xp_ref[...]

    @pl.when(jnp.logical_and(i >= prompt_tiles, i < prompt_tiles + sample_tiles))
    def _():
        xbuf[...] = xs_ref[...]

    @pl.when(i == prompt_tiles + sample_tiles)
    def _():
        xbuf[0:N_META, :] = meta_ref[...]
        xbuf[N_META:ROW_TILE, :] = jnp.zeros((ROW_TILE - N_META, D_MODEL), F32)

    o_ref[...] = _swiglu_half(xbuf[...], g_ref, wg_ref, wu_ref, wd_ref)


def _ffn_final_kernel(x_ref, g_ref, wg_ref, wu_ref, wd_ref, fg_ref, yp_ref, ys_ref,
                      *, prompt_tiles, sample_tiles):
    i = pl.program_id(0)
    y = _rms(_swiglu_half(x_ref[...], g_ref, wg_ref, wu_ref, wd_ref), fg_ref[...])

    @pl.when(i < prompt_tiles)
    def _():
        yp_ref[...] = y

    @pl.when(jnp.logical_and(i >= prompt_tiles, i < prompt_tiles + sample_tiles))
    def _():
        ys_ref[...] = y


def _ffn_weight_specs(layer):
    return [_layer_spec((1, D_MODEL), layer), _layer_spec((D_MODEL, D_FF), layer),
            _layer_spec((D_MODEL, D_FF), layer), _layer_spec((D_FF, D_MODEL), layer)]


def _ffn(x, w, layer):
    n = x.shape[0]
    return pl.pallas_call(
        _ffn_kernel,
        grid=(n // ROW_TILE,),
        in_specs=[_rows_spec(ROW_TILE, D_MODEL)] + _ffn_weight_specs(layer),
        out_specs=_rows_spec(ROW_TILE, D_MODEL),
        out_shape=jax.ShapeDtypeStruct((n, D_MODEL), F32),
        compiler_params=_params(),
        name="ffn",
    )(x, *w)


def _ffn_first(xp, xs, meta, w, layer, lay):
    pt, st = lay["n_prompt"] // ROW_TILE, lay["n_sample"] // ROW_TILE
    n = lay["n_flat"]
    return pl.pallas_call(
        functools.partial(_ffn_first_kernel, prompt_tiles=pt, sample_tiles=st),
        grid=(n // ROW_TILE,),
        in_specs=[pl.BlockSpec((ROW_TILE, D_MODEL), lambda i: (jnp.minimum(i, pt - 1), 0)),
                  pl.BlockSpec((ROW_TILE, D_MODEL), lambda i: (jnp.clip(i - pt, 0, st - 1), 0)),
                  _const_spec((N_META, D_MODEL))] + _ffn_weight_specs(layer),
        out_specs=_rows_spec(ROW_TILE, D_MODEL),
        out_shape=jax.ShapeDtypeStruct((n, D_MODEL), F32),
        scratch_shapes=[pltpu.VMEM((ROW_TILE, D_MODEL), F32)],
        compiler_params=_params(),
        name="ffn_first",
    )(xp, xs, meta, *w)


def _ffn_final(x, w, fg, layer, lay):
    pt, st = lay["n_prompt"] // ROW_TILE, lay["n_sample"] // ROW_TILE
    n = x.shape[0]
    return pl.pallas_call(
        functools.partial(_ffn_final_kernel, prompt_tiles=pt, sample_tiles=st),
        grid=(n // ROW_TILE,),
        in_specs=[_rows_spec(ROW_TILE, D_MODEL)] + _ffn_weight_specs(layer) + [_const_spec((1, D_MODEL))],
        out_specs=[pl.BlockSpec((ROW_TILE, D_MODEL), lambda i: (jnp.minimum(i, pt - 1), 0)),
                   pl.BlockSpec((ROW_TILE, D_MODEL), lambda i: (jnp.clip(i - pt, 0, st - 1), 0))],
        out_shape=[jax.ShapeDtypeStruct((lay["n_prompt"], D_MODEL), F32),
                   jax.ShapeDtypeStruct((lay["n_sample"], D_MODEL), F32)],
        compiler_params=_params(),
        name="ffn_final",
    )(x, *w, fg)


def _glu_kernel(x_ref, g_ref, w_ref, b_ref, u_ref):
    h = _rms(x_ref[...], g_ref[...]).astype(BF16)
    a = _dot(h, w_ref[...]) + b_ref[...]
    u_ref[...] = a[:, :D_MODEL] * jax.nn.sigmoid(a[:, D_MODEL:])


def _glu(x, g, w, b, layer):
    n = x.shape[0]
    return pl.pallas_call(
        _glu_kernel,
        grid=(n // ROW_TILE,),
        in_specs=[_rows_spec(ROW_TILE, D_MODEL), _layer_spec((1, D_MODEL), layer),
                  _layer_spec((D_MODEL, 2 * D_MODEL), layer), _layer_spec((1, 2 * D_MODEL), layer)],
        out_specs=_rows_spec(ROW_TILE, D_MODEL),
        out_shape=jax.ShapeDtypeStruct((n, D_MODEL), F32),
        compiler_params=_params(),
        name="glu",
    )(x, g, w, b)


def _dwconv(ubuf, cbuf, wdw_ref, n_groups, src_row, dst_row):
    for l in range(SLABS):
        lanes = slice(l * LANES, (l + 1) * LANES)

        def body(t, carry):
            src = src_row(t)
            dst = dst_row(t)
            loads = [ubuf[l, pl.ds(src + j, 8, stride=2), :] for j in range(CONV_WIDTH + 1)]
            acc_e = acc_o = None
            for k in range(CONV_WIDTH):
                w = wdw_ref[k:k + 1, lanes]
                te, to = loads[k] * w, loads[k + 1] * w
                acc_e = te if acc_e is None else acc_e + te
                acc_o = to if acc_o is None else acc_o + to
            cbuf[l, pl.ds(dst, 8, stride=2), :] = acc_e
            cbuf[l, pl.ds(dst + 1, 8, stride=2), :] = acc_o
            return carry

        lax.fori_loop(0, n_groups, body, 0, unroll=min(DW_UNROLL, n_groups))


def _conv_tail(cbuf, x, bdw, lng, lnb, w2, b2):
    c = jnp.concatenate([cbuf[l] for l in range(SLABS)], axis=1) + bdw
    mu = jnp.mean(c, axis=-1, keepdims=True)
    cc = c - mu
    var = jnp.mean(cc * cc, axis=-1, keepdims=True)
    y = cc * lax.rsqrt(var + EPS) * lng + lnb
    y = (y * jax.nn.sigmoid(y)).astype(BF16)
    return x + (_dot(y, w2) + b2)


def _conv_prompt_kernel(u_ref, halo_ref, umeta_ref, x_ref, wdw_ref, bdw_ref, lng_ref, lnb_ref,
                        w2_ref, b2_ref, o_ref, ubuf, cbuf, *, tiles_per_stream):
    first = pl.program_id(0) % tiles_per_stream == 0

    @pl.when(first)
    def _():
        for l in range(SLABS):
            ubuf[l, 0:HALO - N_META, :] = jnp.zeros((HALO - N_META, LANES), F32)
            ubuf[l, HALO - N_META:HALO, :] = umeta_ref[:, l * LANES:(l + 1) * LANES]

    @pl.when(jnp.logical_not(first))
    def _():
        for l in range(SLABS):
            ubuf[l, 0:HALO, :] = halo_ref[:, l * LANES:(l + 1) * LANES]

    for l in range(SLABS):
        ubuf[l, HALO:HALO + ROW_TILE, :] = u_ref[:, l * LANES:(l + 1) * LANES]
    _dwconv(ubuf, cbuf, wdw_ref, ROW_TILE // DW_GROUP,
            lambda t: pl.multiple_of(t * DW_GROUP, DW_GROUP) + CONV_BASE,
            lambda t: pl.multiple_of(t * DW_GROUP, DW_GROUP))
    o_ref[...] = _conv_tail(cbuf, x_ref[...], bdw_ref[...], lng_ref[...], lnb_ref[...],
                            w2_ref[...], b2_ref[...])


def _conv_sample_kernel(u_ref, st_ref, x_ref, wdw_ref, bdw_ref, lng_ref, lnb_ref,
                        w2_ref, b2_ref, o_ref, ubuf, cbuf, *, streams):
    span = HALO + CHUNK
    per_stream = CHUNK // DW_GROUP
    for s in range(streams):
        for l in range(SLABS):
            lanes = slice(l * LANES, (l + 1) * LANES)
            ubuf[l, s * span + CONV_BASE:s * span + HALO, :] = st_ref[s, :, lanes]
            ubuf[l, s * span + HALO:(s + 1) * span, :] = u_ref[s * CHUNK:(s + 1) * CHUNK, lanes]
    _dwconv(ubuf, cbuf, wdw_ref, streams * per_stream,
            lambda t: pl.multiple_of((t // per_stream) * span + (t % per_stream) * DW_GROUP, DW_GROUP) + CONV_BASE,
            lambda t: pl.multiple_of(t * DW_GROUP, DW_GROUP))
    o_ref[...] = _conv_tail(cbuf, x_ref[...], bdw_ref[...], lng_ref[...], lnb_ref[...],
                            w2_ref[...], b2_ref[...])


def _conv_meta_kernel(u_ref, x_ref, wdw_ref, bdw_ref, lng_ref, lnb_ref, w2_ref, b2_ref,
                      o_ref, ubuf, cbuf):
    for l in range(SLABS):
        ubuf[l, 0:HALO, :] = jnp.zeros((HALO, LANES), F32)
        ubuf[l, HALO:HALO + N_META, :] = u_ref[:, l * LANES:(l + 1) * LANES]
    _dwconv(ubuf, cbuf, wdw_ref, N_META // DW_GROUP, lambda t: CONV_BASE, lambda t: 0)
    o_ref[...] = _conv_tail(cbuf, x_ref[...], bdw_ref[...], lng_ref[...], lnb_ref[...],
                            w2_ref[...], b2_ref[...])


def _conv_weight_specs(layer):
    return [_layer_spec((CONV_WIDTH, D_MODEL), layer), _layer_spec((1, D_MODEL), layer),
            _layer_spec((1, D_MODEL), layer), _layer_spec((1, D_MODEL), layer),
            _layer_spec((D_MODEL, D_MODEL), layer), _layer_spec((1, D_MODEL), layer)]


def _conv_layer(x, u, state, cw, layer, lay):
    n = x.shape[0]
    out_shape = jax.ShapeDtypeStruct((n, D_MODEL), F32)
    tiles_per_stream = lay["seq"] // ROW_TILE
    n_prompt_tiles = lay["n_prompt"] // ROW_TILE
    meta_blk = lay["meta_off"] // N_META
    slab = lambda rows: pltpu.VMEM((SLABS, rows, LANES), F32)

    x = pl.pallas_call(
        functools.partial(_conv_prompt_kernel, tiles_per_stream=tiles_per_stream),
        grid=(n_prompt_tiles,),
        in_specs=[_rows_spec(ROW_TILE, D_MODEL),
                  pl.BlockSpec((HALO, D_MODEL),
                               lambda i: (jnp.maximum(i * (ROW_TILE // HALO) - 1, 0), 0)),
                  pl.BlockSpec((N_META, D_MODEL), lambda i: (meta_blk, 0)),
                  _rows_spec(ROW_TILE, D_MODEL)] + _conv_weight_specs(layer),
        out_specs=_rows_spec(ROW_TILE, D_MODEL),
        out_shape=out_shape,
        scratch_shapes=[slab(HALO + ROW_TILE), slab(ROW_TILE)],
        input_output_aliases={3: 0},
        compiler_params=_params(),
        name="conv_prompt",
    )(u, u, u, x, *cw)

    streams = ROW_TILE // CHUNK
    x = pl.pallas_call(
        functools.partial(_conv_sample_kernel, streams=streams),
        grid=(lay["n_sample"] // ROW_TILE,),
        in_specs=[_rows_spec(ROW_TILE, D_MODEL, n_prompt_tiles),
                  pl.BlockSpec((None, streams, CONV_WIDTH - 1, D_MODEL), lambda i: (layer, i, 0, 0)),
                  _rows_spec(ROW_TILE, D_MODEL, n_prompt_tiles)] + _conv_weight_specs(layer),
        out_specs=_rows_spec(ROW_TILE, D_MODEL, n_prompt_tiles),
        out_shape=out_shape,
        scratch_shapes=[slab(streams * (HALO + CHUNK)), slab(ROW_TILE)],
        input_output_aliases={2: 0},
        compiler_params=_params(),
        name="conv_sample",
    )(u, state, x, *cw)

    meta_spec = pl.BlockSpec((N_META, D_MODEL), lambda i: (meta_blk, 0))
    x = pl.pallas_call(
        _conv_meta_kernel,
        grid=(1,),
        in_specs=[meta_spec, meta_spec] + _conv_weight_specs(layer),
        out_specs=meta_spec,
        out_shape=out_shape,
        scratch_shapes=[slab(HALO + N_META), slab(N_META)],
        input_output_aliases={1: 0},
        compiler_params=_params(),
        name="conv_meta",
    )(u, x, *cw)
    return x


def _split_dot(a, b):
    hi = a.astype(BF16)
    lo = (a - hi.astype(F32)).astype(BF16)
    return _dot(hi, b) + _dot(lo, b)


def _qkv_kernel(x_ref, g_ref, w_ref, bq_ref, bk_ref, e_ref, gq_ref, gk_ref,
                q_ref, kp_ref, vp_ref, k_ref, v_ref):
    h = _rms(x_ref[...], g_ref[...]).astype(BF16)
    qkv = _dot(h, w_ref[...])
    nq = N_HEADS * HEAD_DIM
    q = qkv[:, :nq]
    k = qkv[:, nq:nq + KV_WIDTH]
    v = qkv[:, nq + KV_WIDTH:]
    q = (q * lax.rsqrt(_split_dot(q * q, bq_ref[...]) + EPS)) * gq_ref[...]
    k = (k * lax.rsqrt(_split_dot(k * k, bk_ref[...]) + EPS)) * gk_ref[...]
    q_ref[...] = (q * (HEAD_DIM ** -0.5)).astype(BF16)
    k_ref[...] = k
    v_ref[...] = v
    kp_ref[...] = _dot(k.astype(BF16), e_ref[...]).astype(BF16)
    vp_ref[...] = _dot(v.astype(BF16), e_ref[...]).astype(BF16)


def _qkv(x, g, w, bq, bk, e, gq, gk, layer):
    n = x.shape[0]
    qkv_w = (N_HEADS + 2 * N_KV_HEADS) * HEAD_DIM
    wide = N_KV_HEADS * PAIR_WIDTH
    return pl.pallas_call(
        _qkv_kernel,
        grid=(n // ROW_TILE,),
        in_specs=[_rows_spec(ROW_TILE, D_MODEL), _layer_spec((1, D_MODEL), layer),
                  _layer_spec((D_MODEL, qkv_w), layer), _const_spec((D_MODEL, D_MODEL)),
                  _const_spec((KV_WIDTH, KV_WIDTH)), _const_spec((KV_WIDTH, wide)),
                  _const_spec((1, D_MODEL)), _const_spec((1, KV_WIDTH))],
        out_specs=[_rows_spec(ROW_TILE, D_MODEL), _rows_spec(ROW_TILE, wide),
                   _rows_spec(ROW_TILE, wide), _rows_spec(ROW_TILE, KV_WIDTH),
                   _rows_spec(ROW_TILE, KV_WIDTH)],
        out_shape=[jax.ShapeDtypeStruct((n, D_MODEL), BF16), jax.ShapeDtypeStruct((n, wide), BF16),
                   jax.ShapeDtypeStruct((n, wide), BF16), jax.ShapeDtypeStruct((n, KV_WIDTH), F32),
                   jax.ShapeDtypeStruct((n, KV_WIDTH), F32)],
        compiler_params=_params(),
        name="qkv",
    )(x, g, w, bq, bk, e, gq, gk)


def _attend(qt, kx, vx, sink_row, bias_col):
    nq = qt.shape[0]
    lane_half = lax.broadcasted_iota(jnp.int32, (nq, PAIR_WIDTH), 1) // HEAD_DIM
    blocks = []
    for g in range(GROUP):
        tile = qt[:, (g // 2) * PAIR_WIDTH:(g // 2 + 1) * PAIR_WIDTH]
        blocks.append(jnp.where(lane_half == g % 2, tile, jnp.zeros((), tile.dtype)))
    ql = jnp.concatenate(blocks, axis=0)
    st = lax.dot_general(kx, ql, (((1,), (1,)), ((), ())), preferred_element_type=F32)
    if bias_col is not None:
        st = st + bias_col
    m = jnp.maximum(jnp.max(st, axis=0, keepdims=True), sink_row)
    p = jnp.exp(st - m)
    den = jnp.sum(p, axis=0, keepdims=True) + jnp.exp(sink_row - m)
    pn = (p * (1.0 / den)).astype(BF16)
    o2 = _dot(pn.T, vx)
    tiles = []
    for j in range(GROUP // 2):
        tiles.append(jnp.where(lane_half == 0, o2[2 * j * nq:(2 * j + 1) * nq],
                               o2[(2 * j + 1) * nq:(2 * j + 2) * nq]))
    return jnp.concatenate(tiles, axis=1)


def _attend_small(qg, kx, vx, sink_col):
    nq = qg.shape[0]
    rows = GROUP * nq
    row_head = lax.broadcasted_iota(jnp.int32, (rows, REP_WIDTH), 0) // nq
    lane_head = lax.broadcasted_iota(jnp.int32, (rows, REP_WIDTH), 1) // HEAD_DIM
    qs = jnp.where(row_head == lane_head, jnp.concatenate([qg] * GROUP, axis=0),
                   jnp.zeros((), qg.dtype))
    s = lax.dot_general(qs, kx, (((1,), (1,)), ((), ())), preferred_element_type=F32)
    m = jnp.maximum(jnp.max(s, axis=-1, keepdims=True), sink_col)
    p = jnp.exp(s - m)
    den = jnp.sum(p, axis=-1, keepdims=True) + jnp.exp(sink_col - m)
    pn = (p * (1.0 / den)).astype(BF16)
    ow = _dot(pn, vx)
    out_head = lax.broadcasted_iota(jnp.int32, (nq, REP_WIDTH), 1) // HEAD_DIM
    o = None
    for g in range(GROUP):
        t = jnp.where(out_head == g, ow[g * nq:(g + 1) * nq, :], 0.0)
        o = t if o is None else o + t
    return o


def _attn_prompt_kernel(q_ref, k_ref, khalo_ref, kmeta_ref, v_ref, vhalo_ref, vmeta_ref,
                        x_ref, sink_ref, wo_ref, o_ref, obuf, *, tiles_per_stream):
    chunks = ROW_TILE // CHUNK
    first_chunk = (pl.program_id(0) % tiles_per_stream) * chunks
    key_pos = lax.broadcasted_iota(jnp.int32, (N_KEYS, 1), 0)

    def band(main_ref, halo_ref, meta_ref, c, lanes):
        parts = []
        for back in (2, 1, 0):
            j = c - back
            if j < 0:
                parts.append(halo_ref[(2 + j) * CHUNK:(3 + j) * CHUNK, lanes])
            else:
                parts.append(main_ref[j * CHUNK:(j + 1) * CHUNK, lanes])
        parts.append(meta_ref[:, lanes])
        return jnp.concatenate(parts, axis=0)

    for c in range(chunks):
        bias = None
        if c < WINDOW // CHUNK:
            missing = jnp.clip(WINDOW // CHUNK - (first_chunk + c), 0, WINDOW // CHUNK) * CHUNK
            bias = jnp.where(key_pos < missing, -jnp.inf, 0.0).astype(F32)
        for kh in range(N_KV_HEADS):
            lanes = slice(kh * PAIR_WIDTH, (kh + 1) * PAIR_WIDTH)
            qlanes = slice(kh * REP_WIDTH, (kh + 1) * REP_WIDTH)
            o = _attend(q_ref[c * CHUNK:(c + 1) * CHUNK, qlanes],
                        band(k_ref, khalo_ref, kmeta_ref, c, lanes),
                        band(v_ref, vhalo_ref, vmeta_ref, c, lanes), sink_ref[kh], bias)
            obuf[c * CHUNK:(c + 1) * CHUNK, qlanes] = o.astype(BF16)
    o_ref[...] = x_ref[...] + _dot(obuf[...], wo_ref[...])


def _attn_sample_kernel(q_ref, k_ref, ck_ref, mk_ref, v_ref, cv_ref, mv_ref,
                        x_ref, sink_ref, wo_ref, o_ref, obuf, *, streams):
    for s in range(streams):
        rows = slice(s * CHUNK, (s + 1) * CHUNK)
        for kh in range(N_KV_HEADS):
            lanes = slice(kh * PAIR_WIDTH, (kh + 1) * PAIR_WIDTH)
            qlanes = slice(kh * REP_WIDTH, (kh + 1) * REP_WIDTH)
            kx = jnp.concatenate([ck_ref[s, :, lanes], k_ref[rows, lanes], mk_ref[s, :, lanes]], axis=0)
            vx = jnp.concatenate([cv_ref[s, :, lanes], v_ref[rows, lanes], mv_ref[s, :, lanes]], axis=0)
            o = _attend(q_ref[rows, qlanes], kx, vx, sink_ref[kh], None)
            obuf[rows, qlanes] = o.astype(BF16)
    o_ref[...] = x_ref[...] + _dot(obuf[...], wo_ref[...])


def _attn_meta_kernel(q_ref, k_ref, v_ref, x_ref, sink_ref, wo_ref, o_ref, obuf):
    for kh in range(N_KV_HEADS):
        lanes = slice(kh * PAIR_WIDTH, (kh + 1) * PAIR_WIDTH)
        qlanes = slice(kh * REP_WIDTH, (kh + 1) * REP_WIDTH)
        kx = jnp.concatenate([k_ref[:, lanes]] * (GROUP // 2), axis=1)
        vx = jnp.concatenate([v_ref[:, lanes]] * (GROUP // 2), axis=1)
        o = _attend_small(q_ref[:, qlanes], kx, vx,
                          sink_ref[kh * GROUP * N_META:(kh + 1) * GROUP * N_META, :])
        obuf[:, qlanes] = o.astype(BF16)
    o_ref[...] = x_ref[...] + _dot(obuf[...], wo_ref[...])


def _attn_layer(x, q, kp, vp, ck, cv, mk, mv, sinks, wo, layer, lay):
    n = x.shape[0]
    out_shape = jax.ShapeDtypeStruct((n, D_MODEL), F32)
    wide = N_KV_HEADS * PAIR_WIDTH
    tiles_per_stream = lay["seq"] // ROW_TILE
    n_prompt_tiles = lay["n_prompt"] // ROW_TILE
    meta_blk = lay["meta_off"] // N_META
    sinks = sinks.astype(F32)
    sink_rows = jnp.repeat(sinks, CHUNK).reshape(N_KV_HEADS, 1, REP_WIDTH)
    sink_meta = jnp.repeat(sinks, N_META)[:, None]
    wo_spec = _layer_spec((D_MODEL, D_MODEL), layer)
    sink_spec = _const_spec((N_KV_HEADS, 1, REP_WIDTH))

    main = _rows_spec(ROW_TILE, wide)
    halo = pl.BlockSpec((WINDOW, wide), lambda i: (jnp.maximum(i * (ROW_TILE // WINDOW) - 1, 0), 0))
    meta = pl.BlockSpec((N_META, wide), lambda i: (meta_blk, 0))
    x = pl.pallas_call(
        functools.partial(_attn_prompt_kernel, tiles_per_stream=tiles_per_stream),
        grid=(n_prompt_tiles,),
        in_specs=[_rows_spec(ROW_TILE, D_MODEL), main, halo, meta, main, halo, meta,
                  _rows_spec(ROW_TILE, D_MODEL), sink_spec, wo_spec],
        out_specs=_rows_spec(ROW_TILE, D_MODEL),
        out_shape=out_shape,
        scratch_shapes=[pltpu.VMEM((ROW_TILE, D_MODEL), BF16)],
        input_output_aliases={7: 0},
        compiler_params=_params(),
        name="attn_prompt",
    )(q, kp, kp, kp, vp, vp, vp, x, sink_rows, wo)

    streams = ROW_TILE // CHUNK
    smain = _rows_spec(ROW_TILE, wide, n_prompt_tiles)
    cache = pl.BlockSpec((None, streams, WINDOW, wide), lambda i: (layer, i, 0, 0))
    mcache = pl.BlockSpec((None, streams, N_META, wide), lambda i: (layer, i, 0, 0))
    x = pl.pallas_call(
        functools.partial(_attn_sample_kernel, streams=streams),
        grid=(lay["n_sample"] // ROW_TILE,),
        in_specs=[_rows_spec(ROW_TILE, D_MODEL, n_prompt_tiles), smain, cache, mcache, smain, cache, mcache,
                  _rows_spec(ROW_TILE, D_MODEL, n_prompt_tiles), sink_spec, wo_spec],
        out_specs=_rows_spec(ROW_TILE, D_MODEL, n_prompt_tiles),
        out_shape=out_shape,
        scratch_shapes=[pltpu.VMEM((ROW_TILE, D_MODEL), BF16)],
        input_output_aliases={7: 0},
        compiler_params=_params(),
        name="attn_sample",
    )(q, kp, ck, mk, vp, cv, mv, x, sink_rows, wo)

    mrow = pl.BlockSpec((N_META, D_MODEL), lambda i: (meta_blk, 0))
    x = pl.pallas_call(
        _attn_meta_kernel,
        grid=(1,),
        in_specs=[mrow, meta, meta, mrow, _const_spec((N_HEADS * N_META, 1)), wo_spec],
        out_specs=mrow,
        out_shape=out_shape,
        scratch_shapes=[pltpu.VMEM((N_META, D_MODEL), BF16)],
        input_output_aliases={3: 0},
        compiler_params=_params(),
        name="attn_meta",
    )(q, kp, vp, x, sink_meta, wo)
    return x


def _head_mean_matrix(width):
    idx = np.arange(width) // HEAD_DIM
    return jnp.asarray((idx[:, None] == idx[None, :]).astype(np.float32) / HEAD_DIM, dtype=BF16)


def _pair_matrix():
    src = np.arange(KV_WIDTH)
    dst = np.arange(N_KV_HEADS * PAIR_WIDTH)
    same_head = (src[:, None] // HEAD_DIM) == (dst[None, :] // PAIR_WIDTH)
    same_dim = (src[:, None] % HEAD_DIM) == (dst[None, :] % HEAD_DIM)
    return jnp.asarray((same_head & same_dim).astype(np.float32), dtype=BF16)


def _pair_heads(c):
    c = c.astype(BF16)
    c = jnp.broadcast_to(c[..., :, None, :], c.shape[:-1] + (2, HEAD_DIM))
    return c.reshape(c.shape[:-3] + (N_KV_HEADS * PAIR_WIDTH,))


def kernel(x_prompt, x_sample, cache_swa_k, cache_swa_v, cache_meta_k, cache_meta_v, state_conv, meta_tokens, ffn1_norm, ffn1_w_gate, ffn1_w_up, ffn1_w_down, ffn2_norm, ffn2_w_gate, ffn2_w_up, ffn2_w_down, conv_norm, conv_w_pw1, conv_b_pw1, conv_w_dw, conv_b_dw, conv_ln_g, conv_ln_b, conv_w_pw2, conv_b_pw2, attn_norm, attn_w_qkv, attn_q_gain, attn_k_gain, attn_sinks, attn_w_o, final_norm):
    batch, seq, _ = x_prompt.shape
    dec_batch, dec_seq, _ = x_sample.shape
    assert seq % ROW_TILE == 0 and dec_seq == CHUNK and (dec_batch * dec_seq) % ROW_TILE == 0
    n_prompt = batch * seq
    n_sample = dec_batch * dec_seq
    meta_off = n_prompt + n_sample
    lay = dict(seq=seq, n_prompt=n_prompt, n_sample=n_sample, meta_off=meta_off,
               n_flat=meta_off + ROW_TILE)

    rows3 = lambda a: a.astype(F32)[:, None, :]
    ffn1 = (rows3(ffn1_norm), ffn1_w_gate.astype(BF16), ffn1_w_up.astype(BF16), ffn1_w_down.astype(BF16))
    ffn2 = (rows3(ffn2_norm), ffn2_w_gate.astype(BF16), ffn2_w_up.astype(BF16), ffn2_w_down.astype(BF16))
    conv_w = (conv_w_dw.astype(F32), rows3(conv_b_dw), rows3(conv_ln_g), rows3(conv_ln_b),
              conv_w_pw2.astype(BF16), rows3(conv_b_pw2))
    glu_w = (rows3(conv_norm), conv_w_pw1.astype(BF16), rows3(conv_b_pw1))
    attn_g, w_qkv, w_o = rows3(attn_norm), attn_w_qkv.astype(BF16), attn_w_o.astype(BF16)
    state = state_conv.astype(F32)
    ck, cv = _pair_heads(cache_swa_k), _pair_heads(cache_swa_v)
    mk, mv = _pair_heads(cache_meta_k), _pair_heads(cache_meta_v)
    bq = _head_mean_matrix(D_MODEL)
    bk = _head_mean_matrix(KV_WIDTH)
    pair = _pair_matrix()
    row = lambda a: a.reshape(1, -1).astype(F32)
    heads = lambda a: a.reshape(a.shape[:-1] + (N_KV_HEADS, HEAD_DIM))
    tail = CONV_WIDTH - 1

    conv_p, conv_s = [], []
    swk_p, swv_p, mk_p, mv_p, swk_s, swv_s = [], [], [], [], [], []
    x = None
    for i in range(DEPTH):
        if i == 0:
            x = _ffn_first(x_prompt.reshape(n_prompt, D_MODEL).astype(F32),
                           x_sample.reshape(n_sample, D_MODEL).astype(F32),
                           meta_tokens.astype(F32), ffn1, i, lay)
        else:
            x = _ffn(x, ffn1, i)
        j = i // 2
        if i % 2 == 0:
            u = _glu(x, *glu_w, j)
            x = _conv_layer(x, u, state, conv_w, j, lay)
            conv_p.append(jnp.stack([u[(b + 1) * seq - tail:(b + 1) * seq] for b in range(batch)]))
            conv_s.append(u[n_prompt:meta_off].reshape(dec_batch, dec_seq, D_MODEL)[:, dec_seq - tail:])
        else:
            q, kp, vp, k, v = _qkv(x, attn_g, w_qkv, bq, bk, pair,
                                   row(jnp.tile(attn_q_gain[j], N_HEADS)),
                                   row(jnp.tile(attn_k_gain[j], N_KV_HEADS)), j)
            x = _attn_layer(x, q, kp, vp, ck, cv, mk, mv, attn_sinks[j], w_o, j, lay)
            swk_p.append(jnp.stack([heads(k[(b + 1) * seq - WINDOW:(b + 1) * seq]) for b in range(batch)]))
            swv_p.append(jnp.stack([heads(v[(b + 1) * seq - WINDOW:(b + 1) * seq]) for b in range(batch)]))
            km, vm = heads(k[meta_off:meta_off + N_META]), heads(v[meta_off:meta_off + N_META])
            mk_p.append(jnp.broadcast_to(km[None], (batch,) + km.shape))
            mv_p.append(jnp.broadcast_to(vm[None], (batch,) + vm.shape))
            ks = heads(k[n_prompt:meta_off].reshape(dec_batch, dec_seq, KV_WIDTH))
            vs = heads(v[n_prompt:meta_off].reshape(dec_batch, dec_seq, KV_WIDTH))
            swk_s.append(jnp.concatenate([cache_swa_k[j].astype(F32)[:, dec_seq:], ks], axis=1))
            swv_s.append(jnp.concatenate([cache_swa_v[j].astype(F32)[:, dec_seq:], vs], axis=1))
        if i < DEPTH - 1:
            x = _ffn(x, ffn2, i)
    y_prompt, y_sample = _ffn_final(x, ffn2, row(final_norm), DEPTH - 1, lay)

    return (y_prompt.reshape(batch, seq, D_MODEL), y_sample.reshape(dec_batch, dec_seq, D_MODEL),
            jnp.stack(swk_p), jnp.stack(swv_p), jnp.stack(mk_p), jnp.stack(mv_p), jnp.stack(conv_p),
            jnp.stack(swk_s), jnp.stack(swv_s), jnp.stack(conv_s))
```

```python
import functools

import numpy as np
import jax
import jax.numpy as jnp
from jax import lax
from jax.experimental import pallas as pl
from jax.experimental.pallas import tpu as pltpu

D_MODEL = 1024
D_FF = 2816
DEPTH = 4
N_META = 16
CHUNK = 64
CONV_WIDTH = 31
HEAD_DIM = 64
N_HEADS = 16
N_KV_HEADS = 2
GROUP = 8
WINDOW = 128
EPS = 1e-6

F32 = jnp.float32
BF16 = jnp.bfloat16

LANES = 128
MXU_TILE = 256
SLABS = D_MODEL // LANES
KV_WIDTH = N_KV_HEADS * HEAD_DIM
PAIR_WIDTH = 2 * HEAD_DIM
REP_WIDTH = GROUP * HEAD_DIM
N_KEYS = WINDOW + CHUNK + N_META
HALO = 32
CONV_BASE = HALO - (CONV_WIDTH - 1)
DW_GROUP = 16

ROW_TILE = 512
FF_SPLITS = ((0, 1024), (1024, 2048), (2048, 2816))

VMEM_LIMIT = 56 * 1024 * 1024


def _params():
    return pltpu.CompilerParams(dimension_semantics=("arbitrary",), vmem_limit_bytes=VMEM_LIMIT)


def _const_spec(shape):
    nd = len(shape)
    return pl.BlockSpec(shape, lambda i: (0,) * nd, pipeline_mode=pl.Buffered(1))


def _layer_spec(shape, layer):
    return pl.BlockSpec((None,) + shape, lambda i: (layer, 0, 0), pipeline_mode=pl.Buffered(1))


def _rows_spec(rows, width, offset_blocks=0):
    return pl.BlockSpec((rows, width), lambda i: (i + offset_blocks, 0))


def _meta_spec(width):
    return pl.BlockSpec((N_META, width), lambda i: (0, 0))


def _rms(x, g):
    ms = jnp.mean(x * x, axis=-1, keepdims=True)
    return (x * lax.rsqrt(ms + EPS)) * g


def _dot(a, b):
    return jnp.dot(a, b, preferred_element_type=F32)


def _tile_and_meta(x_ref, xm_ref, body, out_refs, meta_refs):
    first = pl.program_id(0) == 0

    @pl.when(first)
    def _():
        outs = body(jnp.concatenate([x_ref[...], xm_ref[...]], axis=0))
        for val, o_ref, m_ref in zip(outs, out_refs, meta_refs):
            o_ref[...] = val[:ROW_TILE]
            if m_ref is not None:
                m_ref[...] = val[ROW_TILE:]

    @pl.when(jnp.logical_not(first))
    def _():
        for val, o_ref in zip(body(x_ref[...]), out_refs):
            o_ref[...] = val


def _swiglu_half(x, g_ref, wg_ref, wu_ref, wd_ref):
    h = _rms(x, g_ref[...]).astype(BF16)
    acc = None
    for c0, c1 in FF_SPLITS:
        gate = _dot(h, wg_ref[:, c0:c1])
        up = _dot(h, wu_ref[:, c0:c1])
        a = (gate * jax.nn.sigmoid(gate) * up).astype(BF16)
        part = _dot(a, wd_ref[c0:c1, :])
        acc = part if acc is None else acc + part
    return x + 0.5 * acc


def _ffn_kernel(x_ref, xm_ref, g_ref, wg_ref, wu_ref, wd_ref, o_ref, om_ref):
    body = lambda x: (_swiglu_half(x, g_ref, wg_ref, wu_ref, wd_ref),)
    _tile_and_meta(x_ref, xm_ref, body, (o_ref,), (om_ref,))


def _ffn_first_kernel(xp_ref, xs_ref, xm_ref, g_ref, wg_ref, wu_ref, wd_ref, o_ref, om_ref, xbuf,
                      *, prompt_tiles):
    i = pl.program_id(0)

    @pl.when(i < prompt_tiles)
    def _():
        xbuf[...] = xp_ref[...]

    @pl.when(i >= prompt_tiles)
    def _():
        xbuf[...] = xs_ref[...]

    body = lambda x: (_swiglu_half(x, g_ref, wg_ref, wu_ref, wd_ref),)
    _tile_and_meta(xbuf, xm_ref, body, (o_ref,), (om_ref,))


def _ffn_final_kernel(x_ref, xm_ref, g_ref, wg_ref, wu_ref, wd_ref, fg_ref, yp_ref, ys_ref, ybuf,
                      *, prompt_tiles):
    i = pl.program_id(0)
    body = lambda x: (_rms(_swiglu_half(x, g_ref, wg_ref, wu_ref, wd_ref), fg_ref[...]),)
    _tile_and_meta(x_ref, xm_ref, body, (ybuf,), (None,))

    @pl.when(i < prompt_tiles)
    def _():
        yp_ref[...] = ybuf[...]

    @pl.when(i >= prompt_tiles)
    def _():
        ys_ref[...] = ybuf[...]


def _ffn_weight_specs(layer):
    return [_layer_spec((1, D_MODEL), layer), _layer_spec((D_MODEL, D_FF), layer),
            _layer_spec((D_MODEL, D_FF), layer), _layer_spec((D_FF, D_MODEL), layer)]


def _flat_out(n, width, dtype):
    return ([_rows_spec(ROW_TILE, width), _meta_spec(width)],
            [jax.ShapeDtypeStruct((n, width), dtype), jax.ShapeDtypeStruct((N_META, width), dtype)])


def _ffn(x, xm, w, layer):
    n = x.shape[0]
    out_specs, out_shape = _flat_out(n, D_MODEL, F32)
    return pl.pallas_call(
        _ffn_kernel,
        grid=(n // ROW_TILE,),
        in_specs=[_rows_spec(ROW_TILE, D_MODEL), _meta_spec(D_MODEL)] + _ffn_weight_specs(layer),
        out_specs=out_specs,
        out_shape=out_shape,
        compiler_params=_params(),
        name="ffn",
    )(x, xm, *w)


def _ffn_first(xp, xs, xm, w, layer, lay):
    pt, st = lay["n_prompt"] // ROW_TILE, lay["n_sample"] // ROW_TILE
    n = lay["n_prompt"] + lay["n_sample"]
    out_specs, out_shape = _flat_out(n, D_MODEL, F32)
    return pl.pallas_call(
        functools.partial(_ffn_first_kernel, prompt_tiles=pt),
        grid=(n // ROW_TILE,),
        in_specs=[pl.BlockSpec((ROW_TILE, D_MODEL), lambda i: (jnp.minimum(i, pt - 1), 0)),
                  pl.BlockSpec((ROW_TILE, D_MODEL), lambda i: (jnp.clip(i - pt, 0, st - 1), 0)),
                  _meta_spec(D_MODEL)] + _ffn_weight_specs(layer),
        out_specs=out_specs,
        out_shape=out_shape,
        scratch_shapes=[pltpu.VMEM((ROW_TILE, D_MODEL), F32)],
        compiler_params=_params(),
        name="ffn_first",
    )(xp, xs, xm, *w)


def _ffn_final(x, xm, w, fg, layer, lay):
    pt, st = lay["n_prompt"] // ROW_TILE, lay["n_sample"] // ROW_TILE
    n = x.shape[0]
    return pl.pallas_call(
        functools.partial(_ffn_final_kernel, prompt_tiles=pt),
        grid=(n // ROW_TILE,),
        in_specs=[_rows_spec(ROW_TILE, D_MODEL), _meta_spec(D_MODEL)] + _ffn_weight_specs(layer)
        + [_const_spec((1, D_MODEL))],
        out_specs=[pl.BlockSpec((ROW_TILE, D_MODEL), lambda i: (jnp.minimum(i, pt - 1), 0)),
                   pl.BlockSpec((ROW_TILE, D_MODEL), lambda i: (jnp.clip(i - pt, 0, st - 1), 0))],
        out_shape=[jax.ShapeDtypeStruct((lay["n_prompt"], D_MODEL), F32),
                   jax.ShapeDtypeStruct((lay["n_sample"], D_MODEL), F32)],
        scratch_shapes=[pltpu.VMEM((ROW_TILE, D_MODEL), F32)],
        compiler_params=_params(),
        name="ffn_final",
    )(x, xm, *w, fg)


def _glu_kernel(x_ref, xm_ref, g_ref, w_ref, b_ref, u_ref, um_ref):
    def body(x):
        h = _rms(x, g_ref[...]).astype(BF16)
        a = _dot(h, w_ref[...]) + b_ref[...]
        return (a[:, :D_MODEL] * jax.nn.sigmoid(a[:, D_MODEL:]),)

    _tile_and_meta(x_ref, xm_ref, body, (u_ref,), (um_ref,))


def _glu(x, xm, g, w, b, layer):
    n = x.shape[0]
    out_specs, out_shape = _flat_out(n, D_MODEL, F32)
    return pl.pallas_call(
        _glu_kernel,
        grid=(n // ROW_TILE,),
        in_specs=[_rows_spec(ROW_TILE, D_MODEL), _meta_spec(D_MODEL), _layer_spec((1, D_MODEL), layer),
                  _layer_spec((D_MODEL, 2 * D_MODEL), layer), _layer_spec((1, 2 * D_MODEL), layer)],
        out_specs=out_specs,
        out_shape=out_shape,
        compiler_params=_params(),
        name="glu",
    )(x, xm, g, w, b)


def _dwconv(ubuf, cbuf, wdw_ref, groups):
    for l in range(SLABS):
        lanes = slice(l * LANES, (l + 1) * LANES)
        for src, dst in groups:
            loads = [ubuf[l, pl.ds(src + j, 8, stride=2), :] for j in range(CONV_WIDTH + 1)]
            acc_e = acc_o = None
            for k in range(CONV_WIDTH):
                w = wdw_ref[k:k + 1, lanes]
                te, to = loads[k] * w, loads[k + 1] * w
                acc_e = te if acc_e is None else acc_e + te
                acc_o = to if acc_o is None else acc_o + to
            cbuf[l, pl.ds(dst, 8, stride=2), :] = acc_e
            cbuf[l, pl.ds(dst + 1, 8, stride=2), :] = acc_o


def _conv_tail(cbuf, x, bdw, lng, lnb, w2, b2):
    c = jnp.concatenate([cbuf[l] for l in range(SLABS)], axis=1) + bdw
    mu = jnp.mean(c, axis=-1, keepdims=True)
    cc = c - mu
    var = jnp.mean(cc * cc, axis=-1, keepdims=True)
    y = cc * lax.rsqrt(var + EPS) * lng + lnb
    y = (y * jax.nn.sigmoid(y)).astype(BF16)
    return x + (_dot(y, w2) + b2)


def _conv_prompt_kernel(u_ref, halo_ref, umeta_ref, x_ref, wdw_ref, bdw_ref, lng_ref, lnb_ref,
                        w2_ref, b2_ref, o_ref, ubuf, cbuf, *, tiles_per_stream):
    first = pl.program_id(0) % tiles_per_stream == 0

    @pl.when(first)
    def _():
        for l in range(SLABS):
            ubuf[l, 0:HALO - N_META, :] = jnp.zeros((HALO - N_META, LANES), F32)
            ubuf[l, HALO - N_META:HALO, :] = umeta_ref[:, l * LANES:(l + 1) * LANES]

    @pl.when(jnp.logical_not(first))
    def _():
        for l in range(SLABS):
            ubuf[l, 0:HALO, :] = halo_ref[:, l * LANES:(l + 1) * LANES]

    for l in range(SLABS):
        ubuf[l, HALO:HALO + ROW_TILE, :] = u_ref[:, l * LANES:(l + 1) * LANES]
    _dwconv(ubuf, cbuf, wdw_ref, [(r + CONV_BASE, r) for r in range(0, ROW_TILE, DW_GROUP)])
    o_ref[...] = _conv_tail(cbuf, x_ref[...], bdw_ref[...], lng_ref[...], lnb_ref[...],
                            w2_ref[...], b2_ref[...])


def _conv_sample_kernel(u_ref, st_ref, x_ref, wdw_ref, bdw_ref, lng_ref, lnb_ref,
                        w2_ref, b2_ref, o_ref, ubuf, cbuf, *, streams):
    span = HALO + CHUNK
    for s in range(streams):
        for l in range(SLABS):
            lanes = slice(l * LANES, (l + 1) * LANES)
            ubuf[l, s * span + CONV_BASE:s * span + HALO, :] = st_ref[s, :, lanes]
            ubuf[l, s * span + HALO:(s + 1) * span, :] = u_ref[s * CHUNK:(s + 1) * CHUNK, lanes]
    _dwconv(ubuf, cbuf, wdw_ref, [(s * span + r + CONV_BASE, s * CHUNK + r)
                                  for s in range(streams) for r in range(0, CHUNK, DW_GROUP)])
    o_ref[...] = _conv_tail(cbuf, x_ref[...], bdw_ref[...], lng_ref[...], lnb_ref[...],
                            w2_ref[...], b2_ref[...])


def _conv_meta_kernel(u_ref, x_ref, wdw_ref, bdw_ref, lng_ref, lnb_ref, w2_ref, b2_ref,
                      o_ref, ubuf, cbuf):
    for l in range(SLABS):
        ubuf[l, 0:HALO, :] = jnp.zeros((HALO, LANES), F32)
        ubuf[l, HALO:HALO + N_META, :] = u_ref[:, l * LANES:(l + 1) * LANES]
    _dwconv(ubuf, cbuf, wdw_ref, [(r + CONV_BASE, r) for r in range(0, N_META, DW_GROUP)])
    o_ref[...] = _conv_tail(cbuf, x_ref[...], bdw_ref[...], lng_ref[...], lnb_ref[...],
                            w2_ref[...], b2_ref[...])


def _conv_weight_specs(layer):
    return [_layer_spec((CONV_WIDTH, D_MODEL), layer), _layer_spec((1, D_MODEL), layer),
            _layer_spec((1, D_MODEL), layer), _layer_spec((1, D_MODEL), layer),
            _layer_spec((D_MODEL, D_MODEL), layer), _layer_spec((1, D_MODEL), layer)]


def _conv_layer(x, xm, u, um, state, cw, layer, lay):
    n = x.shape[0]
    out_shape = jax.ShapeDtypeStruct((n, D_MODEL), F32)
    tiles_per_stream = lay["seq"] // ROW_TILE
    n_prompt_tiles = lay["n_prompt"] // ROW_TILE
    slab = lambda rows: pltpu.VMEM((SLABS, rows, LANES), F32)

    x = pl.pallas_call(
        functools.partial(_conv_prompt_kernel, tiles_per_stream=tiles_per_stream),
        grid=(n_prompt_tiles,),
        in_specs=[_rows_spec(ROW_TILE, D_MODEL),
                  pl.BlockSpec((HALO, D_MODEL),
                               lambda i: (jnp.maximum(i * (ROW_TILE // HALO) - 1, 0), 0)),
                  _meta_spec(D_MODEL),
                  _rows_spec(ROW_TILE, D_MODEL)] + _conv_weight_specs(layer),
        out_specs=_rows_spec(ROW_TILE, D_MODEL),
        out_shape=out_shape,
        scratch_shapes=[slab(HALO + ROW_TILE), slab(ROW_TILE)],
        input_output_aliases={3: 0},
        compiler_params=_params(),
        name="conv_prompt",
    )(u, u, um, x, *cw)

    streams = ROW_TILE // CHUNK
    x = pl.pallas_call(
        functools.partial(_conv_sample_kernel, streams=streams),
        grid=(lay["n_sample"] // ROW_TILE,),
        in_specs=[_rows_spec(ROW_TILE, D_MODEL, n_prompt_tiles),
                  pl.BlockSpec((None, streams, CONV_WIDTH - 1, D_MODEL), lambda i: (layer, i, 0, 0)),
                  _rows_spec(ROW_TILE, D_MODEL, n_prompt_tiles)] + _conv_weight_specs(layer),
        out_specs=_rows_spec(ROW_TILE, D_MODEL, n_prompt_tiles),
        out_shape=out_shape,
        scratch_shapes=[slab(streams * (HALO + CHUNK)), slab(ROW_TILE)],
        input_output_aliases={2: 0},
        compiler_params=_params(),
        name="conv_sample",
    )(u, state, x, *cw)

    xm = pl.pallas_call(
        _conv_meta_kernel,
        grid=(1,),
        in_specs=[_meta_spec(D_MODEL), _meta_spec(D_MODEL)] + _conv_weight_specs(layer),
        out_specs=_meta_spec(D_MODEL),
        out_shape=jax.ShapeDtypeStruct((N_META, D_MODEL), F32),
        scratch_shapes=[slab(HALO + N_META), slab(N_META)],
        compiler_params=_params(),
        name="conv_meta",
    )(um, xm, *cw)
    return x, xm


def _split_dot(a, b):
    hi = a.astype(BF16)
    lo = (a - hi.astype(F32)).astype(BF16)
    return _dot(hi, b) + _dot(lo, b)


def _head_mean(sq, b_ref):
    w = b_ref.shape[0]
    parts = [_split_dot(sq[:, j * w:(j + 1) * w], b_ref[...]) for j in range(sq.shape[1] // w)]
    return parts[0] if len(parts) == 1 else jnp.concatenate(parts, axis=1)


def _qkv_kernel(x_ref, xm_ref, g_ref, w_ref, bq_ref, bk_ref, e_ref, gq_ref, gk_ref,
                q_ref, kp_ref, vp_ref, k_ref, v_ref, qm_ref, kpm_ref, vpm_ref, km_ref, vm_ref):
    def body(x):
        h = _rms(x, g_ref[...]).astype(BF16)
        qkv = _dot(h, w_ref[...])
        nq = N_HEADS * HEAD_DIM
        q = qkv[:, :nq]
        k = qkv[:, nq:nq + KV_WIDTH]
        v = qkv[:, nq + KV_WIDTH:]
        q = (q * lax.rsqrt(_head_mean(q * q, bq_ref) + EPS)) * gq_ref[...]
        k = (k * lax.rsqrt(_head_mean(k * k, bk_ref) + EPS)) * gk_ref[...]
        return ((q * (HEAD_DIM ** -0.5)).astype(BF16),
                _dot(k.astype(BF16), e_ref[...]).astype(BF16),
                _dot(v.astype(BF16), e_ref[...]).astype(BF16), k, v)

    _tile_and_meta(x_ref, xm_ref, body, (q_ref, kp_ref, vp_ref, k_ref, v_ref),
                   (qm_ref, kpm_ref, vpm_ref, km_ref, vm_ref))


def _qkv(x, xm, g, w, bq, bk, e, gq, gk, layer):
    n = x.shape[0]
    qkv_w = (N_HEADS + 2 * N_KV_HEADS) * HEAD_DIM
    wide = N_KV_HEADS * PAIR_WIDTH
    outs = [(D_MODEL, BF16), (wide, BF16), (wide, BF16), (KV_WIDTH, F32), (KV_WIDTH, F32)]
    return pl.pallas_call(
        _qkv_kernel,
        grid=(n // ROW_TILE,),
        in_specs=[_rows_spec(ROW_TILE, D_MODEL), _meta_spec(D_MODEL), _layer_spec((1, D_MODEL), layer),
                  _layer_spec((D_MODEL, qkv_w), layer), _const_spec((MXU_TILE, MXU_TILE)),
                  _const_spec((KV_WIDTH, KV_WIDTH)), _const_spec((KV_WIDTH, wide)),
                  _const_spec((1, D_MODEL)), _const_spec((1, KV_WIDTH))],
        out_specs=[_rows_spec(ROW_TILE, w_) for w_, _ in outs] + [_meta_spec(w_) for w_, _ in outs],
        out_shape=[jax.ShapeDtypeStruct((n, w_), d_) for w_, d_ in outs]
        + [jax.ShapeDtypeStruct((N_META, w_), d_) for w_, d_ in outs],
        compiler_params=_params(),
        name="qkv",
    )(x, xm, g, w, bq, bk, e, gq, gk)


def _attend(qt, kx, vx, sink_row, bias_col):
    nq = qt.shape[0]
    lane_half = lax.broadcasted_iota(jnp.int32, (nq, PAIR_WIDTH), 1) // HEAD_DIM
    blocks = []
    for g in range(GROUP):
        tile = qt[:, (g // 2) * PAIR_WIDTH:(g // 2 + 1) * PAIR_WIDTH]
        blocks.append(jnp.where(lane_half == g % 2, tile, jnp.zeros((), tile.dtype)))
    ql = jnp.concatenate(blocks, axis=0)
    st = lax.dot_general(kx, ql, (((1,), (1,)), ((), ())), preferred_element_type=F32)
    if bias_col is not None:
        st = st + bias_col
    m = jnp.maximum(jnp.max(st, axis=0, keepdims=True), sink_row)
    p = jnp.exp(st - m)
    den = jnp.sum(p, axis=0, keepdims=True) + jnp.exp(sink_row - m)
    pn = (p * (1.0 / den)).astype(BF16)
    o2 = _dot(pn.T, vx)
    tiles = []
    for j in range(GROUP // 2):
        tiles.append(jnp.where(lane_half == 0, o2[2 * j * nq:(2 * j + 1) * nq],
                               o2[(2 * j + 1) * nq:(2 * j + 2) * nq]))
    return jnp.concatenate(tiles, axis=1)


def _attend_small(qg, kx, vx, sink_col):
    nq = qg.shape[0]
    rows = GROUP * nq
    row_head = lax.broadcasted_iota(jnp.int32, (rows, REP_WIDTH), 0) // nq
    lane_head = lax.broadcasted_iota(jnp.int32, (rows, REP_WIDTH), 1) // HEAD_DIM
    qs = jnp.where(row_head == lane_head, jnp.concatenate([qg] * GROUP, axis=0),
                   jnp.zeros((), qg.dtype))
    s = lax.dot_general(qs, kx, (((1,), (1,)), ((), ())), preferred_element_type=F32)
    m = jnp.maximum(jnp.max(s, axis=-1, keepdims=True), sink_col)
    p = jnp.exp(s - m)
    den = jnp.sum(p, axis=-1, keepdims=True) + jnp.exp(sink_col - m)
    pn = (p * (1.0 / den)).astype(BF16)
    ow = _dot(pn, vx)
    out_head = lax.broadcasted_iota(jnp.int32, (nq, REP_WIDTH), 1) // HEAD_DIM
    o = None
    for g in range(GROUP):
        t = jnp.where(out_head == g, ow[g * nq:(g + 1) * nq, :], 0.0)
        o = t if o is None else o + t
    return o


def _attn_prompt_kernel(q_ref, k_ref, khalo_ref, kmeta_ref, v_ref, vhalo_ref, vmeta_ref,
                        x_ref, sink_ref, wo_ref, o_ref, obuf, *, tiles_per_stream):
    chunks = ROW_TILE // CHUNK
    first_chunk = (pl.program_id(0) % tiles_per_stream) * chunks
    key_pos = lax.broadcasted_iota(jnp.int32, (N_KEYS, 1), 0)

    def band(main_ref, halo_ref, meta_ref, c, lanes):
        parts = []
        for back in (2, 1, 0):
            j = c - back
            if j < 0:
                parts.append(halo_ref[(2 + j) * CHUNK:(3 + j) * CHUNK, lanes])
            else:
                parts.append(main_ref[j * CHUNK:(j + 1) * CHUNK, lanes])
        parts.append(meta_ref[:, lanes])
        return jnp.concatenate(parts, axis=0)

    for c in range(chunks):
        bias = None
        if c < WINDOW // CHUNK:
            missing = jnp.clip(WINDOW // CHUNK - (first_chunk + c), 0, WINDOW // CHUNK) * CHUNK
            bias = jnp.where(key_pos < missing, -jnp.inf, 0.0).astype(F32)
        for kh in range(N_KV_HEADS):
            lanes = slice(kh * PAIR_WIDTH, (kh + 1) * PAIR_WIDTH)
            qlanes = slice(kh * REP_WIDTH, (kh + 1) * REP_WIDTH)
            o = _attend(q_ref[c * CHUNK:(c + 1) * CHUNK, qlanes],
                        band(k_ref, khalo_ref, kmeta_ref, c, lanes),
                        band(v_ref, vhalo_ref, vmeta_ref, c, lanes), sink_ref[kh], bias)
            obuf[c * CHUNK:(c + 1) * CHUNK, qlanes] = o.astype(BF16)
    o_ref[...] = x_ref[...] + _dot(obuf[...], wo_ref[...])


def _attn_sample_kernel(q_ref, k_ref, ck_ref, mk_ref, v_ref, cv_ref, mv_ref,
                        x_ref, sink_ref, wo_ref, o_ref, obuf, *, streams):
    for s in range(streams):
        rows = slice(s * CHUNK, (s + 1) * CHUNK)
        for kh in range(N_KV_HEADS):
            lanes = slice(kh * PAIR_WIDTH, (kh + 1) * PAIR_WIDTH)
            qlanes = slice(kh * REP_WIDTH, (kh + 1) * REP_WIDTH)
            kx = jnp.concatenate([ck_ref[s, :, lanes], k_ref[rows, lanes], mk_ref[s, :, lanes]], axis=0)
            vx = jnp.concatenate([cv_ref[s, :, lanes], v_ref[rows, lanes], mv_ref[s, :, lanes]], axis=0)
            o = _attend(q_ref[rows, qlanes], kx, vx, sink_ref[kh], None)
            obuf[rows, qlanes] = o.astype(BF16)
    o_ref[...] = x_ref[...] + _dot(obuf[...], wo_ref[...])


def _attn_meta_kernel(q_ref, k_ref, v_ref, x_ref, sink_ref, wo_ref, o_ref, obuf):
    for kh in range(N_KV_HEADS):
        lanes = slice(kh * PAIR_WIDTH, (kh + 1) * PAIR_WIDTH)
        qlanes = slice(kh * REP_WIDTH, (kh + 1) * REP_WIDTH)
        kx = jnp.concatenate([k_ref[:, lanes]] * (GROUP // 2), axis=1)
        vx = jnp.concatenate([v_ref[:, lanes]] * (GROUP // 2), axis=1)
        o = _attend_small(q_ref[:, qlanes], kx, vx,
                          sink_ref[kh * GROUP * N_META:(kh + 1) * GROUP * N_META, :])
        obuf[:, qlanes] = o.astype(BF16)
    o_ref[...] = x_ref[...] + _dot(obuf[...], wo_ref[...])


def _attn_layer(x, xm, q, kp, vp, qm, kpm, vpm, ck, cv, mk, mv, sinks, wo, layer, lay):
    n = x.shape[0]
    out_shape = jax.ShapeDtypeStruct((n, D_MODEL), F32)
    wide = N_KV_HEADS * PAIR_WIDTH
    tiles_per_stream = lay["seq"] // ROW_TILE
    n_prompt_tiles = lay["n_prompt"] // ROW_TILE
    sinks = sinks.astype(F32)
    sink_rows = jnp.repeat(sinks, CHUNK).reshape(N_KV_HEADS, 1, REP_WIDTH)
    sink_meta = jnp.repeat(sinks, N_META)[:, None]
    wo_spec = _layer_spec((D_MODEL, D_MODEL), layer)
    sink_spec = _const_spec((N_KV_HEADS, 1, REP_WIDTH))

    main = _rows_spec(ROW_TILE, wide)
    halo = pl.BlockSpec((WINDOW, wide), lambda i: (jnp.maximum(i * (ROW_TILE // WINDOW) - 1, 0), 0))
    x = pl.pallas_call(
        functools.partial(_attn_prompt_kernel, tiles_per_stream=tiles_per_stream),
        grid=(n_prompt_tiles,),
        in_specs=[_rows_spec(ROW_TILE, D_MODEL), main, halo, _meta_spec(wide), main, halo, _meta_spec(wide),
                  _rows_spec(ROW_TILE, D_MODEL), sink_spec, wo_spec],
        out_specs=_rows_spec(ROW_TILE, D_MODEL),
        out_shape=out_shape,
        scratch_shapes=[pltpu.VMEM((ROW_TILE, D_MODEL), BF16)],
        input_output_aliases={7: 0},
        compiler_params=_params(),
        name="attn_prompt",
    )(q, kp, kp, kpm, vp, vp, vpm, x, sink_rows, wo)

    streams = ROW_TILE // CHUNK
    smain = _rows_spec(ROW_TILE, wide, n_prompt_tiles)
    cache = pl.BlockSpec((None, streams, WINDOW, wide), lambda i: (layer, i, 0, 0))
    mcache = pl.BlockSpec((None, streams, N_META, wide), lambda i: (layer, i, 0, 0))
    x = pl.pallas_call(
        functools.partial(_attn_sample_kernel, streams=streams),
        grid=(lay["n_sample"] // ROW_TILE,),
        in_specs=[_rows_spec(ROW_TILE, D_MODEL, n_prompt_tiles), smain, cache, mcache, smain, cache, mcache,
                  _rows_spec(ROW_TILE, D_MODEL, n_prompt_tiles), sink_spec, wo_spec],
        out_specs=_rows_spec(ROW_TILE, D_MODEL, n_prompt_tiles),
        out_shape=out_shape,
        scratch_shapes=[pltpu.VMEM((ROW_TILE, D_MODEL), BF16)],
        input_output_aliases={7: 0},
        compiler_params=_params(),
        name="attn_sample",
    )(q, kp, ck, mk, vp, cv, mv, x, sink_rows, wo)

    xm = pl.pallas_call(
        _attn_meta_kernel,
        grid=(1,),
        in_specs=[_meta_spec(D_MODEL), _meta_spec(wide), _meta_spec(wide), _meta_spec(D_MODEL),
                  _const_spec((N_HEADS * N_META, 1)), wo_spec],
        out_specs=_meta_spec(D_MODEL),
        out_shape=jax.ShapeDtypeStruct((N_META, D_MODEL), F32),
        scratch_shapes=[pltpu.VMEM((N_META, D_MODEL), BF16)],
        compiler_params=_params(),
        name="attn_meta",
    )(qm, kpm, vpm, xm, sink_meta, wo)
    return x, xm


def _head_mean_matrix(width):
    idx = np.arange(width) // HEAD_DIM
    return jnp.asarray((idx[:, None] == idx[None, :]).astype(np.float32) / HEAD_DIM, dtype=BF16)


def _pair_matrix():
    src = np.arange(KV_WIDTH)
    dst = np.arange(N_KV_HEADS * PAIR_WIDTH)
    same_head = (src[:, None] // HEAD_DIM) == (dst[None, :] // PAIR_WIDTH)
    same_dim = (src[:, None] % HEAD_DIM) == (dst[None, :] % HEAD_DIM)
    return jnp.asarray((same_head & same_dim).astype(np.float32), dtype=BF16)


def _pair_heads(c):
    c = c.astype(BF16)
    c = jnp.broadcast_to(c[..., :, None, :], c.shape[:-1] + (2, HEAD_DIM))
    return c.reshape(c.shape[:-3] + (N_KV_HEADS * PAIR_WIDTH,))


def kernel(x_prompt, x_sample, cache_swa_k, cache_swa_v, cache_meta_k, cache_meta_v, state_conv, meta_tokens, ffn1_norm, ffn1_w_gate, ffn1_w_up, ffn1_w_down, ffn2_norm, ffn2_w_gate, ffn2_w_up, ffn2_w_down, conv_norm, conv_w_pw1, conv_b_pw1, conv_w_dw, conv_b_dw, conv_ln_g, conv_ln_b, conv_w_pw2, conv_b_pw2, attn_norm, attn_w_qkv, attn_q_gain, attn_k_gain, attn_sinks, attn_w_o, final_norm):
    batch, seq, _ = x_prompt.shape
    dec_batch, dec_seq, _ = x_sample.shape
    assert seq % ROW_TILE == 0 and dec_seq == CHUNK and (dec_batch * dec_seq) % ROW_TILE == 0
    n_prompt = batch * seq
    n_sample = dec_batch * dec_seq
    n_flat = n_prompt + n_sample
    lay = dict(seq=seq, n_prompt=n_prompt, n_sample=n_sample)

    rows3 = lambda a: a.astype(F32)[:, None, :]
    ffn1 = (rows3(ffn1_norm), ffn1_w_gate.astype(BF16), ffn1_w_up.astype(BF16), ffn1_w_down.astype(BF16))
    ffn2 = (rows3(ffn2_norm), ffn2_w_gate.astype(BF16), ffn2_w_up.astype(BF16), ffn2_w_down.astype(BF16))
    conv_w = (conv_w_dw.astype(F32), rows3(conv_b_dw), rows3(conv_ln_g), rows3(conv_ln_b),
              conv_w_pw2.astype(BF16), rows3(conv_b_pw2))
    glu_w = (rows3(conv_norm), conv_w_pw1.astype(BF16), rows3(conv_b_pw1))
    attn_g, w_qkv, w_o = rows3(attn_norm), attn_w_qkv.astype(BF16), attn_w_o.astype(BF16)
    state = state_conv.astype(F32)
    ck, cv = _pair_heads(cache_swa_k), _pair_heads(cache_swa_v)
    mk, mv = _pair_heads(cache_meta_k), _pair_heads(cache_meta_v)
    bq = _head_mean_matrix(MXU_TILE)
    bk = _head_mean_matrix(KV_WIDTH)
    pair = _pair_matrix()
    row = lambda a: a.reshape(1, -1).astype(F32)
    heads = lambda a: a.reshape(a.shape[:-1] + (N_KV_HEADS, HEAD_DIM))
    tail = CONV_WIDTH - 1

    conv_p, conv_s = [], []
    swk_p, swv_p, mk_p, mv_p, swk_s, swv_s = [], [], [], [], [], []
    x = xm = None
    for i in range(DEPTH):
        if i == 0:
            x, xm = _ffn_first(x_prompt.reshape(n_prompt, D_MODEL).astype(F32),
                               x_sample.reshape(n_sample, D_MODEL).astype(F32),
                               meta_tokens.astype(F32), ffn1, i, lay)
        else:
            x, xm = _ffn(x, xm, ffn1, i)
        j = i // 2
        if i % 2 == 0:
            u, um = _glu(x, xm, *glu_w, j)
            x, xm = _conv_layer(x, xm, u, um, state, conv_w, j, lay)
            conv_p.append(jnp.stack([u[(b + 1) * seq - tail:(b + 1) * seq] for b in range(batch)]))
            conv_s.append(u[n_prompt:n_flat].reshape(dec_batch, dec_seq, D_MODEL)[:, dec_seq - tail:])
        else:
            q, kp, vp, k, v, qm, kpm, vpm, km, vm = _qkv(
                x, xm, attn_g, w_qkv, bq, bk, pair, row(jnp.tile(attn_q_gain[j], N_HEADS)),
                row(jnp.tile(attn_k_gain[j], N_KV_HEADS)), j)
            x, xm = _attn_layer(x, xm, q, kp, vp, qm, kpm, vpm, ck, cv, mk, mv,
                                attn_sinks[j], w_o, j, lay)
            swk_p.append(jnp.stack([heads(k[(b + 1) * seq - WINDOW:(b + 1) * seq]) for b in range(batch)]))
            swv_p.append(jnp.stack([heads(v[(b + 1) * seq - WINDOW:(b + 1) * seq]) for b in range(batch)]))
            mk_p.append(jnp.broadcast_to(heads(km)[None], (batch, N_META, N_KV_HEADS, HEAD_DIM)))
            mv_p.append(jnp.broadcast_to(heads(vm)[None], (batch, N_META, N_KV_HEADS, HEAD_DIM)))
            ks = heads(k[n_prompt:n_flat].reshape(dec_batch, dec_seq, KV_WIDTH))
            vs = heads(v[n_prompt:n_flat].reshape(dec_batch, dec_seq, KV_WIDTH))
            swk_s.append(jnp.concatenate([cache_swa_k[j].astype(F32)[:, dec_seq:], ks], axis=1))
            swv_s.append(jnp.concatenate([cache_swa_v[j].astype(F32)[:, dec_seq:], vs], axis=1))
        if i < DEPTH - 1:
            x, xm = _ffn(x, xm, ffn2, i)
    y_prompt, y_sample = _ffn_final(x, xm, ffn2, row(final_norm), DEPTH - 1, lay)

    return (y_prompt.reshape(batch, seq, D_MODEL), y_sample.reshape(dec_batch, dec_seq, D_MODEL),
            jnp.stack(swk_p), jnp.stack(swv_p), jnp.stack(mk_p), jnp.stack(mv_p), jnp.stack(conv_p),
            jnp.stack(swk_s), jnp.stack(swv_s), jnp.stack(conv_s))
```

```python
import functools

import numpy as np
import jax
import jax.numpy as jnp
from jax import lax
from jax.experimental import pallas as pl
from jax.experimental.pallas import tpu as pltpu

D_MODEL = 1024
D_FF = 2816
DEPTH = 4
N_META = 16
CHUNK = 64
CONV_WIDTH = 31
HEAD_DIM = 64
N_HEADS = 16
N_KV_HEADS = 2
GROUP = 8
WINDOW = 128
EPS = 1e-6

F32 = jnp.float32
BF16 = jnp.bfloat16

LANES = 128
MXU_TILE = 256
SLABS = D_MODEL // LANES
KV_WIDTH = N_KV_HEADS * HEAD_DIM
PAIR_WIDTH = 2 * HEAD_DIM
REP_WIDTH = GROUP * HEAD_DIM
N_KEYS = WINDOW + CHUNK + N_META
HALO = 32
CONV_BASE = HALO - (CONV_WIDTH - 1)
DW_GROUP = 16

ROW_TILE = 512
CAST_GU_STEPS = 32
CAST_DN_STEPS = 16
FF_SPLITS = ((0, 1024), (1024, 2048), (2048, 2816))

VMEM_LIMIT = 56 * 1024 * 1024


def _params():
    return pltpu.CompilerParams(dimension_semantics=("arbitrary",), vmem_limit_bytes=VMEM_LIMIT)


def _const_spec(shape):
    nd = len(shape)
    return pl.BlockSpec(shape, lambda i: (0,) * nd, pipeline_mode=pl.Buffered(1))


def _layer_spec(shape, layer):
    return pl.BlockSpec((None,) + shape, lambda i: (layer, 0, 0), pipeline_mode=pl.Buffered(1))


def _rows_spec(rows, width, offset_blocks=0):
    return pl.BlockSpec((rows, width), lambda i: (i + offset_blocks, 0))


def _meta_spec(width):
    return pl.BlockSpec((N_META, width), lambda i: (0, 0))


def _rms(x, g):
    ms = jnp.mean(x * x, axis=-1, keepdims=True)
    return (x * lax.rsqrt(ms + EPS)) * g


def _dot(a, b):
    return jnp.dot(a, b, preferred_element_type=F32)


def _tile_and_meta(x_ref, xm_ref, body, out_refs, meta_refs):
    first = pl.program_id(0) == 0

    @pl.when(first)
    def _():
        outs = body(jnp.concatenate([x_ref[...], xm_ref[...]], axis=0))
        for val, o_ref, m_ref in zip(outs, out_refs, meta_refs):
            o_ref[...] = val[:ROW_TILE]
            if m_ref is not None:
                m_ref[...] = val[ROW_TILE:]

    @pl.when(jnp.logical_not(first))
    def _():
        for val, o_ref in zip(body(x_ref[...]), out_refs):
            o_ref[...] = val


def _swiglu_half(x, g_ref, wg_ref, wu_ref, wd_ref):
    h = _rms(x, g_ref[...]).astype(BF16)
    acc = None
    for c0, c1 in FF_SPLITS:
        gate = _dot(h, wg_ref[:, c0:c1])
        up = _dot(h, wu_ref[:, c0:c1])
        a = (gate * jax.nn.sigmoid(gate) * up).astype(BF16)
        part = _dot(a, wd_ref[c0:c1, :])
        acc = part if acc is None else acc + part
    return x + 0.5 * acc


def _cast_next(next_refs, out_refs):
    for n_ref, o_ref in zip(next_refs, out_refs):
        o_ref[...] = n_ref[...].astype(BF16)


def _ffn_kernel(x_ref, xm_ref, g_ref, wg_ref, wu_ref, wd_ref, ng_ref, nu_ref, nd_ref,
                o_ref, om_ref, og_ref, ou_ref, od_ref):
    _cast_next((ng_ref, nu_ref, nd_ref), (og_ref, ou_ref, od_ref))
    body = lambda x: (_swiglu_half(x, g_ref, wg_ref, wu_ref, wd_ref),)
    _tile_and_meta(x_ref, xm_ref, body, (o_ref,), (om_ref,))


def _ffn_first_kernel(xp_ref, xs_ref, xm_ref, g_ref, wg_ref, wu_ref, wd_ref, ng_ref, nu_ref, nd_ref,
                      o_ref, om_ref, og_ref, ou_ref, od_ref, xbuf, *, prompt_tiles):
    i = pl.program_id(0)
    _cast_next((ng_ref, nu_ref, nd_ref), (og_ref, ou_ref, od_ref))

    @pl.when(i < prompt_tiles)
    def _():
        xbuf[...] = xp_ref[...]

    @pl.when(i >= prompt_tiles)
    def _():
        xbuf[...] = xs_ref[...]

    body = lambda x: (_swiglu_half(x, g_ref, wg_ref, wu_ref, wd_ref),)
    _tile_and_meta(xbuf, xm_ref, body, (o_ref,), (om_ref,))


def _ffn_final_kernel(x_ref, xm_ref, g_ref, wg_ref, wu_ref, wd_ref, fg_ref, yp_ref, ys_ref, ybuf,
                      *, prompt_tiles):
    i = pl.program_id(0)
    body = lambda x: (_rms(_swiglu_half(x, g_ref, wg_ref, wu_ref, wd_ref), fg_ref[...]),)
    _tile_and_meta(x_ref, xm_ref, body, (ybuf,), (None,))

    @pl.when(i < prompt_tiles)
    def _():
        yp_ref[...] = ybuf[...]

    @pl.when(i >= prompt_tiles)
    def _():
        ys_ref[...] = ybuf[...]


def _ffn_weight_specs(layer):
    return [_layer_spec((1, D_MODEL), layer), _const_spec((D_MODEL, D_FF)),
            _const_spec((D_MODEL, D_FF)), _const_spec((D_FF, D_MODEL))]


def _next_weight_specs(layer, steps):
    assert steps >= max(CAST_GU_STEPS, CAST_DN_STEPS)
    gu_rows, dn_rows = D_MODEL // CAST_GU_STEPS, D_FF // CAST_DN_STEPS
    gu_i = lambda i: jnp.minimum(i, CAST_GU_STEPS - 1)
    dn_i = lambda i: jnp.minimum(i, CAST_DN_STEPS - 1)
    gu_in = pl.BlockSpec((None, gu_rows, D_FF), lambda i: (layer, gu_i(i), 0))
    dn_in = pl.BlockSpec((None, dn_rows, D_MODEL), lambda i: (layer, dn_i(i), 0))
    gu_out = pl.BlockSpec((gu_rows, D_FF), lambda i: (gu_i(i), 0))
    dn_out = pl.BlockSpec((dn_rows, D_MODEL), lambda i: (dn_i(i), 0))
    shapes = [jax.ShapeDtypeStruct((D_MODEL, D_FF), BF16), jax.ShapeDtypeStruct((D_MODEL, D_FF), BF16),
              jax.ShapeDtypeStruct((D_FF, D_MODEL), BF16)]
    return [gu_in, gu_in, dn_in], [gu_out, gu_out, dn_out], shapes


def _flat_out(n, width, dtype):
    return ([_rows_spec(ROW_TILE, width), _meta_spec(width)],
            [jax.ShapeDtypeStruct((n, width), dtype), jax.ShapeDtypeStruct((N_META, width), dtype)])


def _ffn(x, xm, g, w, layer, nxt, nxt_layer):
    n = x.shape[0]
    out_specs, out_shape = _flat_out(n, D_MODEL, F32)
    n_in, n_out, n_shape = _next_weight_specs(nxt_layer, n // ROW_TILE)
    x, xm, *w_next = pl.pallas_call(
        _ffn_kernel,
        grid=(n // ROW_TILE,),
        in_specs=[_rows_spec(ROW_TILE, D_MODEL), _meta_spec(D_MODEL)] + _ffn_weight_specs(layer) + n_in,
        out_specs=out_specs + n_out,
        out_shape=out_shape + n_shape,
        compiler_params=_params(),
        name="ffn",
    )(x, xm, g, *w, *nxt)
    return x, xm, w_next


def _ffn_first(xp, xs, xm, g, w, layer, nxt, nxt_layer, lay):
    pt, st = lay["n_prompt"] // ROW_TILE, lay["n_sample"] // ROW_TILE
    n = lay["n_prompt"] + lay["n_sample"]
    out_specs, out_shape = _flat_out(n, D_MODEL, F32)
    n_in, n_out, n_shape = _next_weight_specs(nxt_layer, n // ROW_TILE)
    x, xm, *w_next = pl.pallas_call(
        functools.partial(_ffn_first_kernel, prompt_tiles=pt),
        grid=(n // ROW_TILE,),
        in_specs=[pl.BlockSpec((ROW_TILE, D_MODEL), lambda i: (jnp.minimum(i, pt - 1), 0)),
                  pl.BlockSpec((ROW_TILE, D_MODEL), lambda i: (jnp.clip(i - pt, 0, st - 1), 0)),
                  _meta_spec(D_MODEL)] + _ffn_weight_specs(layer) + n_in,
        out_specs=out_specs + n_out,
        out_shape=out_shape + n_shape,
        scratch_shapes=[pltpu.VMEM((ROW_TILE, D_MODEL), F32)],
        compiler_params=_params(),
        name="ffn_first",
    )(xp, xs, xm, g, *w, *nxt)
    return x, xm, w_next


def _ffn_final(x, xm, g, w, fg, layer, lay):
    pt, st = lay["n_prompt"] // ROW_TILE, lay["n_sample"] // ROW_TILE
    n = x.shape[0]
    return pl.pallas_call(
        functools.partial(_ffn_final_kernel, prompt_tiles=pt),
        grid=(n // ROW_TILE,),
        in_specs=[_rows_spec(ROW_TILE, D_MODEL), _meta_spec(D_MODEL)] + _ffn_weight_specs(layer)
        + [_const_spec((1, D_MODEL))],
        out_specs=[pl.BlockSpec((ROW_TILE, D_MODEL), lambda i: (jnp.minimum(i, pt - 1), 0)),
                   pl.BlockSpec((ROW_TILE, D_MODEL), lambda i: (jnp.clip(i - pt, 0, st - 1), 0))],
        out_shape=[jax.ShapeDtypeStruct((lay["n_prompt"], D_MODEL), F32),
                   jax.ShapeDtypeStruct((lay["n_sample"], D_MODEL), F32)],
        scratch_shapes=[pltpu.VMEM((ROW_TILE, D_MODEL), F32)],
        compiler_params=_params(),
        name="ffn_final",
    )(x, xm, g, *w, fg)


def _glu_kernel(x_ref, xm_ref, g_ref, w_ref, b_ref, u_ref, um_ref):
    def body(x):
        h = _rms(x, g_ref[...]).astype(BF16)
        a = _dot(h, w_ref[...]) + b_ref[...]
        return (a[:, :D_MODEL] * jax.nn.sigmoid(a[:, D_MODEL:]),)

    _tile_and_meta(x_ref, xm_ref, body, (u_ref,), (um_ref,))


def _glu(x, xm, g, w, b, layer):
    n = x.shape[0]
    out_specs, out_shape = _flat_out(n, D_MODEL, F32)
    return pl.pallas_call(
        _glu_kernel,
        grid=(n // ROW_TILE,),
        in_specs=[_rows_spec(ROW_TILE, D_MODEL), _meta_spec(D_MODEL), _layer_spec((1, D_MODEL), layer),
                  _layer_spec((D_MODEL, 2 * D_MODEL), layer), _layer_spec((1, 2 * D_MODEL), layer)],
        out_specs=out_specs,
        out_shape=out_shape,
        compiler_params=_params(),
        name="glu",
    )(x, xm, g, w, b)


def _dwconv(ubuf, cbuf, wdw_ref, groups):
    for l in range(SLABS):
        lanes = slice(l * LANES, (l + 1) * LANES)
        for src, dst in groups:
            loads = [ubuf[l, pl.ds(src + j, 8, stride=2), :] for j in range(CONV_WIDTH + 1)]
            acc_e = acc_o = None
            for k in range(CONV_WIDTH):
                w = wdw_ref[k:k + 1, lanes]
                te, to = loads[k] * w, loads[k + 1] * w
                acc_e = te if acc_e is None else acc_e + te
                acc_o = to if acc_o is None else acc_o + to
            cbuf[l, pl.ds(dst, 8, stride=2), :] = acc_e
            cbuf[l, pl.ds(dst + 1, 8, stride=2), :] = acc_o


def _conv_tail(cbuf, x, bdw, lng, lnb, w2, b2):
    c = jnp.concatenate([cbuf[l] for l in range(SLABS)], axis=1) + bdw
    mu = jnp.mean(c, axis=-1, keepdims=True)
    cc = c - mu
    var = jnp.mean(cc * cc, axis=-1, keepdims=True)
    y = cc * lax.rsqrt(var + EPS) * lng + lnb
    y = (y * jax.nn.sigmoid(y)).astype(BF16)
    return x + (_dot(y, w2) + b2)


def _conv_prompt_kernel(u_ref, halo_ref, umeta_ref, x_ref, wdw_ref, bdw_ref, lng_ref, lnb_ref,
                        w2_ref, b2_ref, o_ref, ubuf, cbuf, *, tiles_per_stream):
    first = pl.program_id(0) % tiles_per_stream == 0

    @pl.when(first)
    def _():
        for l in range(SLABS):
            ubuf[l, 0:HALO - N_META, :] = jnp.zeros((HALO - N_META, LANES), F32)
            ubuf[l, HALO - N_META:HALO, :] = umeta_ref[:, l * LANES:(l + 1) * LANES]

    @pl.when(jnp.logical_not(first))
    def _():
        for l in range(SLABS):
            ubuf[l, 0:HALO, :] = halo_ref[:, l * LANES:(l + 1) * LANES]

    for l in range(SLABS):
        ubuf[l, HALO:HALO + ROW_TILE, :] = u_ref[:, l * LANES:(l + 1) * LANES]
    _dwconv(ubuf, cbuf, wdw_ref, [(r + CONV_BASE, r) for r in range(0, ROW_TILE, DW_GROUP)])
    o_ref[...] = _conv_tail(cbuf, x_ref[...], bdw_ref[...], lng_ref[...], lnb_ref[...],
                            w2_ref[...], b2_ref[...])


def _conv_sample_kernel(u_ref, st_ref, x_ref, wdw_ref, bdw_ref, lng_ref, lnb_ref,
                        w2_ref, b2_ref, o_ref, ubuf, cbuf, *, streams):
    span = HALO + CHUNK
    for s in range(streams):
        for l in range(SLABS):
            lanes = slice(l * LANES, (l + 1) * LANES)
            ubuf[l, s * span + CONV_BASE:s * span + HALO, :] = st_ref[s, :, lanes]
            ubuf[l, s * span + HALO:(s + 1) * span, :] = u_ref[s * CHUNK:(s + 1) * CHUNK, lanes]
    _dwconv(ubuf, cbuf, wdw_ref, [(s * span + r + CONV_BASE, s * CHUNK + r)
                                  for s in range(streams) for r in range(0, CHUNK, DW_GROUP)])
    o_ref[...] = _conv_tail(cbuf, x_ref[...], bdw_ref[...], lng_ref[...], lnb_ref[...],
                            w2_ref[...], b2_ref[...])


def _conv_meta_kernel(u_ref, x_ref, wdw_ref, bdw_ref, lng_ref, lnb_ref, w2_ref, b2_ref,
                      o_ref, ubuf, cbuf):
    for l in range(SLABS):
        ubuf[l, 0:HALO, :] = jnp.zeros((HALO, LANES), F32)
        ubuf[l, HALO:HALO + N_META, :] = u_ref[:, l * LANES:(l + 1) * LANES]
    _dwconv(ubuf, cbuf, wdw_ref, [(r + CONV_BASE, r) for r in range(0, N_META, DW_GROUP)])
    o_ref[...] = _conv_tail(cbuf, x_ref[...], bdw_ref[...], lng_ref[...], lnb_ref[...],
                            w2_ref[...], b2_ref[...])


def _conv_weight_specs(layer):
    return [_layer_spec((CONV_WIDTH, D_MODEL), layer), _layer_spec((1, D_MODEL), layer),
            _layer_spec((1, D_MODEL), layer), _layer_spec((1, D_MODEL), layer),
            _layer_spec((D_MODEL, D_MODEL), layer), _layer_spec((1, D_MODEL), layer)]


def _conv_layer(x, xm, u, um, state, cw, layer, lay):
    n = x.shape[0]
    out_shape = jax.ShapeDtypeStruct((n, D_MODEL), F32)
    tiles_per_stream = lay["seq"] // ROW_TILE
    n_prompt_tiles = lay["n_prompt"] // ROW_TILE
    slab = lambda rows: pltpu.VMEM((SLABS, rows, LANES), F32)

    x = pl.pallas_call(
        functools.partial(_conv_prompt_kernel, tiles_per_stream=tiles_per_stream),
        grid=(n_prompt_tiles,),
        in_specs=[_rows_spec(ROW_TILE, D_MODEL),
                  pl.BlockSpec((HALO, D_MODEL),
                               lambda i: (jnp.maximum(i * (ROW_TILE // HALO) - 1, 0), 0)),
                  _meta_spec(D_MODEL),
                  _rows_spec(ROW_TILE, D_MODEL)] + _conv_weight_specs(layer),
        out_specs=_rows_spec(ROW_TILE, D_MODEL),
        out_shape=out_shape,
        scratch_shapes=[slab(HALO + ROW_TILE), slab(ROW_TILE)],
        input_output_aliases={3: 0},
        compiler_params=_params(),
        name="conv_prompt",
    )(u, u, um, x, *cw)

    streams = ROW_TILE // CHUNK
    x = pl.pallas_call(
        functools.partial(_conv_sample_kernel, streams=streams),
        grid=(lay["n_sample"] // ROW_TILE,),
        in_specs=[_rows_spec(ROW_TILE, D_MODEL, n_prompt_tiles),
                  pl.BlockSpec((None, streams, CONV_WIDTH - 1, D_MODEL), lambda i: (layer, i, 0, 0)),
                  _rows_spec(ROW_TILE, D_MODEL, n_prompt_tiles)] + _conv_weight_specs(layer),
        out_specs=_rows_spec(ROW_TILE, D_MODEL, n_prompt_tiles),
        out_shape=out_shape,
        scratch_shapes=[slab(streams * (HALO + CHUNK)), slab(ROW_TILE)],
        input_output_aliases={2: 0},
        compiler_params=_params(),
        name="conv_sample",
    )(u, state, x, *cw)

    xm = pl.pallas_call(
        _conv_meta_kernel,
        grid=(1,),
        in_specs=[_meta_spec(D_MODEL), _meta_spec(D_MODEL)] + _conv_weight_specs(layer),
        out_specs=_meta_spec(D_MODEL),
        out_shape=jax.ShapeDtypeStruct((N_META, D_MODEL), F32),
        scratch_shapes=[slab(HALO + N_META), slab(N_META)],
        compiler_params=_params(),
        name="conv_meta",
    )(um, xm, *cw)
    return x, xm


def _split_dot(a, b):
    hi = a.astype(BF16)
    lo = (a - hi.astype(F32)).astype(BF16)
    return _dot(hi, b) + _dot(lo, b)


def _head_mean(sq, b_ref):
    w = b_ref.shape[0]
    parts = [_split_dot(sq[:, j * w:(j + 1) * w], b_ref[...]) for j in range(sq.shape[1] // w)]
    return parts[0] if len(parts) == 1 else jnp.concatenate(parts, axis=1)


def _qkv_kernel(x_ref, xm_ref, g_ref, w_ref, bq_ref, bk_ref, e_ref, gq_ref, gk_ref,
                q_ref, kp_ref, vp_ref, k_ref, v_ref, qm_ref, kpm_ref, vpm_ref, km_ref, vm_ref):
    def body(x):
        h = _rms(x, g_ref[...]).astype(BF16)
        qkv = _dot(h, w_ref[...])
        nq = N_HEADS * HEAD_DIM
        q = qkv[:, :nq]
        k = qkv[:, nq:nq + KV_WIDTH]
        v = qkv[:, nq + KV_WIDTH:]
        q = (q * lax.rsqrt(_head_mean(q * q, bq_ref) + EPS)) * gq_ref[...]
        k = (k * lax.rsqrt(_head_mean(k * k, bk_ref) + EPS)) * gk_ref[...]
        return ((q * (HEAD_DIM ** -0.5)).astype(BF16),
                _dot(k.astype(BF16), e_ref[...]).astype(BF16),
                _dot(v.astype(BF16), e_ref[...]).astype(BF16), k, v)

    _tile_and_meta(x_ref, xm_ref, body, (q_ref, kp_ref, vp_ref, k_ref, v_ref),
                   (qm_ref, kpm_ref, vpm_ref, km_ref, vm_ref))


def _qkv(x, xm, g, w, bq, bk, e, gq, gk, layer):
    n = x.shape[0]
    qkv_w = (N_HEADS + 2 * N_KV_HEADS) * HEAD_DIM
    wide = N_KV_HEADS * PAIR_WIDTH
    outs = [(D_MODEL, BF16), (wide, BF16), (wide, BF16), (KV_WIDTH, F32), (KV_WIDTH, F32)]
    return pl.pallas_call(
        _qkv_kernel,
        grid=(n // ROW_TILE,),
        in_specs=[_rows_spec(ROW_TILE, D_MODEL), _meta_spec(D_MODEL), _layer_spec((1, D_MODEL), layer),
                  _layer_spec((D_MODEL, qkv_w), layer), _const_spec((MXU_TILE, MXU_TILE)),
                  _const_spec((KV_WIDTH, KV_WIDTH)), _const_spec((KV_WIDTH, wide)),
                  _const_spec((1, D_MODEL)), _const_spec((1, KV_WIDTH))],
        out_specs=[_rows_spec(ROW_TILE, w_) for w_, _ in outs] + [_meta_spec(w_) for w_, _ in outs],
        out_shape=[jax.ShapeDtypeStruct((n, w_), d_) for w_, d_ in outs]
        + [jax.ShapeDtypeStruct((N_META, w_), d_) for w_, d_ in outs],
        compiler_params=_params(),
        name="qkv",
    )(x, xm, g, w, bq, bk, e, gq, gk)


def _attend(qt, kx, vx, sink_row, bias_col):
    nq = qt.shape[0]
    lane_half = lax.broadcasted_iota(jnp.int32, (nq, PAIR_WIDTH), 1) // HEAD_DIM
    zero = jnp.zeros((nq, PAIR_WIDTH), qt.dtype)
    blocks = []
    for h in range(N_HEADS):
        kh, g = divmod(h, GROUP)
        lo = kh * REP_WIDTH + (g // 2) * PAIR_WIDTH
        own = jnp.where(lane_half == g % 2, qt[:, lo:lo + PAIR_WIDTH], jnp.zeros((), qt.dtype))
        blocks.append(jnp.concatenate([own if c == kh else zero for c in range(N_KV_HEADS)], axis=1))
    ql = jnp.concatenate(blocks, axis=0)
    st = lax.dot_general(kx, ql, (((1,), (1,)), ((), ())), preferred_element_type=F32)
    if bias_col is not None:
        st = st + bias_col
    m = jnp.maximum(jnp.max(st, axis=0, keepdims=True), sink_row)
    p = jnp.exp(st - m)
    den = jnp.sum(p, axis=0, keepdims=True) + jnp.exp(sink_row - m)
    pn = (p * (1.0 / den)).astype(BF16)
    o2 = _dot(pn.T, vx)
    tiles = []
    for h in range(0, N_HEADS, 2):
        lanes = slice((h // GROUP) * PAIR_WIDTH, (h // GROUP + 1) * PAIR_WIDTH)
        tiles.append(jnp.where(lane_half == 0, o2[h * nq:(h + 1) * nq, lanes],
                               o2[(h + 1) * nq:(h + 2) * nq, lanes]))
    return jnp.concatenate(tiles, axis=1)


def _attend_small(qg, kx, vx, sink_col):
    nq = qg.shape[0]
    rows = GROUP * nq
    row_head = lax.broadcasted_iota(jnp.int32, (rows, REP_WIDTH), 0) // nq
    lane_head = lax.broadcasted_iota(jnp.int32, (rows, REP_WIDTH), 1) // HEAD_DIM
    qs = jnp.where(row_head == lane_head, jnp.concatenate([qg] * GROUP, axis=0),
                   jnp.zeros((), qg.dtype))
    s = lax.dot_general(qs, kx, (((1,), (1,)), ((), ())), preferred_element_type=F32)
    m = jnp.maximum(jnp.max(s, axis=-1, keepdims=True), sink_col)
    p = jnp.exp(s - m)
    den = jnp.sum(p, axis=-1, keepdims=True) + jnp.exp(sink_col - m)
    pn = (p * (1.0 / den)).astype(BF16)
    ow = _dot(pn, vx)
    out_head = lax.broadcasted_iota(jnp.int32, (nq, REP_WIDTH), 1) // HEAD_DIM
    o = None
    for g in range(GROUP):
        t = jnp.where(out_head == g, ow[g * nq:(g + 1) * nq, :], 0.0)
        o = t if o is None else o + t
    return o


def _attn_prompt_kernel(q_ref, k_ref, khalo_ref, kmeta_ref, v_ref, vhalo_ref, vmeta_ref,
                        x_ref, sink_ref, wo_ref, o_ref, obuf, *, tiles_per_stream):
    chunks = ROW_TILE // CHUNK
    first_chunk = (pl.program_id(0) % tiles_per_stream) * chunks
    key_pos = lax.broadcasted_iota(jnp.int32, (N_KEYS, 1), 0)

    def band(main_ref, halo_ref, meta_ref, c):
        parts = []
        for back in (2, 1, 0):
            j = c - back
            if j < 0:
                parts.append(halo_ref[(2 + j) * CHUNK:(3 + j) * CHUNK, :])
            else:
                parts.append(main_ref[j * CHUNK:(j + 1) * CHUNK, :])
        parts.append(meta_ref[...])
        return jnp.concatenate(parts, axis=0)

    for c in range(chunks):
        bias = None
        if c < WINDOW // CHUNK:
            missing = jnp.clip(WINDOW // CHUNK - (first_chunk + c), 0, WINDOW // CHUNK) * CHUNK
            bias = jnp.where(key_pos < missing, -jnp.inf, 0.0).astype(F32)
        rows = slice(c * CHUNK, (c + 1) * CHUNK)
        o = _attend(q_ref[rows, :], band(k_ref, khalo_ref, kmeta_ref, c),
                    band(v_ref, vhalo_ref, vmeta_ref, c), sink_ref[...], bias)
        obuf[rows, :] = o.astype(BF16)
    o_ref[...] = x_ref[...] + _dot(obuf[...], wo_ref[...])


def _attn_sample_kernel(q_ref, k_ref, ck_ref, mk_ref, v_ref, cv_ref, mv_ref,
                        x_ref, sink_ref, wo_ref, o_ref, obuf, *, streams):
    for s in range(streams):
        rows = slice(s * CHUNK, (s + 1) * CHUNK)
        kx = jnp.concatenate([ck_ref[s], k_ref[rows, :], mk_ref[s]], axis=0)
        vx = jnp.concatenate([cv_ref[s], v_ref[rows, :], mv_ref[s]], axis=0)
        obuf[rows, :] = _attend(q_ref[rows, :], kx, vx, sink_ref[...], None).astype(BF16)
    o_ref[...] = x_ref[...] + _dot(obuf[...], wo_ref[...])


def _attn_meta_kernel(q_ref, k_ref, v_ref, x_ref, sink_ref, wo_ref, o_ref, obuf):
    for kh in range(N_KV_HEADS):
        lanes = slice(kh * PAIR_WIDTH, (kh + 1) * PAIR_WIDTH)
        qlanes = slice(kh * REP_WIDTH, (kh + 1) * REP_WIDTH)
        kx = jnp.concatenate([k_ref[:, lanes]] * (GROUP // 2), axis=1)
        vx = jnp.concatenate([v_ref[:, lanes]] * (GROUP // 2), axis=1)
        o = _attend_small(q_ref[:, qlanes], kx, vx,
                          sink_ref[kh * GROUP * N_META:(kh + 1) * GROUP * N_META, :])
        obuf[:, qlanes] = o.astype(BF16)
    o_ref[...] = x_ref[...] + _dot(obuf[...], wo_ref[...])


def _attn_layer(x, xm, q, kp, vp, qm, kpm, vpm, ck, cv, mk, mv, sinks, wo, layer, lay):
    n = x.shape[0]
    out_shape = jax.ShapeDtypeStruct((n, D_MODEL), F32)
    wide = N_KV_HEADS * PAIR_WIDTH
    tiles_per_stream = lay["seq"] // ROW_TILE
    n_prompt_tiles = lay["n_prompt"] // ROW_TILE
    sinks = sinks.astype(F32)
    sink_rows = jnp.repeat(sinks, CHUNK).reshape(1, N_HEADS * CHUNK)
    sink_meta = jnp.repeat(sinks, N_META)[:, None]
    wo_spec = _layer_spec((D_MODEL, D_MODEL), layer)
    sink_spec = _const_spec((1, N_HEADS * CHUNK))

    main = _rows_spec(ROW_TILE, wide)
    halo = pl.BlockSpec((WINDOW, wide), lambda i: (jnp.maximum(i * (ROW_TILE // WINDOW) - 1, 0), 0))
    x = pl.pallas_call(
        functools.partial(_attn_prompt_kernel, tiles_per_stream=tiles_per_stream),
        grid=(n_prompt_tiles,),
        in_specs=[_rows_spec(ROW_TILE, D_MODEL), main, halo, _meta_spec(wide), main, halo, _meta_spec(wide),
                  _rows_spec(ROW_TILE, D_MODEL), sink_spec, wo_spec],
        out_specs=_rows_spec(ROW_TILE, D_MODEL),
        out_shape=out_shape,
        scratch_shapes=[pltpu.VMEM((ROW_TILE, D_MODEL), BF16)],
        input_output_aliases={7: 0},
        compiler_params=_params(),
        name="attn_prompt",
    )(q, kp, kp, kpm, vp, vp, vpm, x, sink_rows, wo)

    streams = ROW_TILE // CHUNK
    smain = _rows_spec(ROW_TILE, wide, n_prompt_tiles)
    cache = pl.BlockSpec((None, streams, WINDOW, wide), lambda i: (layer, i, 0, 0))
    mcache = pl.BlockSpec((None, streams, N_META, wide), lambda i: (layer, i, 0, 0))
    x = pl.pallas_call(
        functools.partial(_attn_sample_kernel, streams=streams),
        grid=(lay["n_sample"] // ROW_TILE,),
        in_specs=[_rows_spec(ROW_TILE, D_MODEL, n_prompt_tiles), smain, cache, mcache, smain, cache, mcache,
                  _rows_spec(ROW_TILE, D_MODEL, n_prompt_tiles), sink_spec, wo_spec],
        out_specs=_rows_spec(ROW_TILE, D_MODEL, n_prompt_tiles),
        out_shape=out_shape,
        scratch_shapes=[pltpu.VMEM((ROW_TILE, D_MODEL), BF16)],
        input_output_aliases={7: 0},
        compiler_params=_params(),
        name="attn_sample",
    )(q, kp, ck, mk, vp, cv, mv, x, sink_rows, wo)

    xm = pl.pallas_call(
        _attn_meta_kernel,
        grid=(1,),
        in_specs=[_meta_spec(D_MODEL), _meta_spec(wide), _meta_spec(wide), _meta_spec(D_MODEL),
                  _const_spec((N_HEADS * N_META, 1)), wo_spec],
        out_specs=_meta_spec(D_MODEL),
        out_shape=jax.ShapeDtypeStruct((N_META, D_MODEL), F32),
        scratch_shapes=[pltpu.VMEM((N_META, D_MODEL), BF16)],
        compiler_params=_params(),
        name="attn_meta",
    )(qm, kpm, vpm, xm, sink_meta, wo)
    return x, xm


def _head_mean_matrix(width):
    idx = np.arange(width) // HEAD_DIM
    return jnp.asarray((idx[:, None] == idx[None, :]).astype(np.float32) / HEAD_DIM, dtype=BF16)


def _pair_matrix():
    src = np.arange(KV_WIDTH)
    dst = np.arange(N_KV_HEADS * PAIR_WIDTH)
    same_head = (src[:, None] // HEAD_DIM) == (dst[None, :] // PAIR_WIDTH)
    same_dim = (src[:, None] % HEAD_DIM) == (dst[None, :] % HEAD_DIM)
    return jnp.asarray((same_head & same_dim).astype(np.float32), dtype=BF16)


def _pair_heads(c):
    c = c.astype(BF16)
    c = jnp.broadcast_to(c[..., :, None, :], c.shape[:-1] + (2, HEAD_DIM))
    return c.reshape(c.shape[:-3] + (N_KV_HEADS * PAIR_WIDTH,))


def kernel(x_prompt, x_sample, cache_swa_k, cache_swa_v, cache_meta_k, cache_meta_v, state_conv, meta_tokens, ffn1_norm, ffn1_w_gate, ffn1_w_up, ffn1_w_down, ffn2_norm, ffn2_w_gate, ffn2_w_up, ffn2_w_down, conv_norm, conv_w_pw1, conv_b_pw1, conv_w_dw, conv_b_dw, conv_ln_g, conv_ln_b, conv_w_pw2, conv_b_pw2, attn_norm, attn_w_qkv, attn_q_gain, attn_k_gain, attn_sinks, attn_w_o, final_norm):
    batch, seq, _ = x_prompt.shape
    dec_batch, dec_seq, _ = x_sample.shape
    assert seq % ROW_TILE == 0 and dec_seq == CHUNK and (dec_batch * dec_seq) % ROW_TILE == 0
    n_prompt = batch * seq
    n_sample = dec_batch * dec_seq
    n_flat = n_prompt + n_sample
    lay = dict(seq=seq, n_prompt=n_prompt, n_sample=n_sample)

    rows3 = lambda a: a.astype(F32)[:, None, :]
    ffn_f32 = ((ffn1_w_gate, ffn1_w_up, ffn1_w_down), (ffn2_w_gate, ffn2_w_up, ffn2_w_down))
    ffn_g = (rows3(ffn1_norm), rows3(ffn2_norm))
    w_cur = [a[0].astype(BF16) for a in ffn_f32[0]]
    conv_w = (conv_w_dw.astype(F32), rows3(conv_b_dw), rows3(conv_ln_g), rows3(conv_ln_b),
              conv_w_pw2.astype(BF16), rows3(conv_b_pw2))
    glu_w = (rows3(conv_norm), conv_w_pw1.astype(BF16), rows3(conv_b_pw1))
    attn_g, w_qkv, w_o = rows3(attn_norm), attn_w_qkv.astype(BF16), attn_w_o.astype(BF16)
    state = state_conv.astype(F32)
    ck, cv = _pair_heads(cache_swa_k), _pair_heads(cache_swa_v)
    mk, mv = _pair_heads(cache_meta_k), _pair_heads(cache_meta_v)
    bq = _head_mean_matrix(MXU_TILE)
    bk = _head_mean_matrix(KV_WIDTH)
    pair = _pair_matrix()
    row = lambda a: a.reshape(1, -1).astype(F32)
    heads = lambda a: a.reshape(a.shape[:-1] + (N_KV_HEADS, HEAD_DIM))
    tail = CONV_WIDTH - 1

    conv_p, conv_s = [], []
    swk_p, swv_p, mk_p, mv_p, swk_s, swv_s = [], [], [], [], [], []
    x = xm = None
    for i in range(DEPTH):
        if i == 0:
            x, xm, w_cur = _ffn_first(x_prompt.reshape(n_prompt, D_MODEL).astype(F32),
                                      x_sample.reshape(n_sample, D_MODEL).astype(F32),
                                      meta_tokens.astype(F32), ffn_g[0], w_cur, i, ffn_f32[1], i, lay)
        else:
            x, xm, w_cur = _ffn(x, xm, ffn_g[0], w_cur, i, ffn_f32[1], i)
        j = i // 2
        if i % 2 == 0:
            u, um = _glu(x, xm, *glu_w, j)
            x, xm = _conv_layer(x, xm, u, um, state, conv_w, j, lay)
            conv_p.append(jnp.stack([u[(b + 1) * seq - tail:(b + 1) * seq] for b in range(batch)]))
            conv_s.append(u[n_prompt:n_flat].reshape(dec_batch, dec_seq, D_MODEL)[:, dec_seq - tail:])
        else:
            q, kp, vp, k, v, qm, kpm, vpm, km, vm = _qkv(
                x, xm, attn_g, w_qkv, bq, bk, pair, row(jnp.tile(attn_q_gain[j], N_HEADS)),
                row(jnp.tile(attn_k_gain[j], N_KV_HEADS)), j)
            x, xm = _attn_layer(x, xm, q, kp, vp, qm, kpm, vpm, ck, cv, mk, mv,
                                attn_sinks[j], w_o, j, lay)
            swk_p.append(jnp.stack([heads(k[(b + 1) * seq - WINDOW:(b + 1) * seq]) for b in range(batch)]))
            swv_p.append(jnp.stack([heads(v[(b + 1) * seq - WINDOW:(b + 1) * seq]) for b in range(batch)]))
            mk_p.append(jnp.broadcast_to(heads(km)[None], (batch, N_META, N_KV_HEADS, HEAD_DIM)))
            mv_p.append(jnp.broadcast_to(heads(vm)[None], (batch, N_META, N_KV_HEADS, HEAD_DIM)))
            ks = heads(k[n_prompt:n_flat].reshape(dec_batch, dec_seq, KV_WIDTH))
            vs = heads(v[n_prompt:n_flat].reshape(dec_batch, dec_seq, KV_WIDTH))
            swk_s.append(jnp.concatenate([cache_swa_k[j].astype(F32)[:, dec_seq:], ks], axis=1))
            swv_s.append(jnp.concatenate([cache_swa_v[j].astype(F32)[:, dec_seq:], vs], axis=1))
        if i < DEPTH - 1:
            x, xm, w_cur = _ffn(x, xm, ffn_g[1], w_cur, i, ffn_f32[0], i + 1)
    y_prompt, y_sample = _ffn_final(x, xm, ffn_g[1], w_cur, row(final_norm), DEPTH - 1, lay)

    return (y_prompt.reshape(batch, seq, D_MODEL), y_sample.reshape(dec_batch, dec_seq, D_MODEL),
            jnp.stack(swk_p), jnp.stack(swv_p), jnp.stack(mk_p), jnp.stack(mv_p), jnp.stack(conv_p),
            jnp.stack(swk_s), jnp.stack(swv_s), jnp.stack(conv_s))
```

```python
import functools

import numpy as np
import jax
import jax.numpy as jnp
from jax import lax
from jax.experimental import pallas as pl
from jax.experimental.pallas import tpu as pltpu

D_MODEL = 1024
D_FF = 2816
DEPTH = 4
N_META = 16
CHUNK = 64
CONV_WIDTH = 31
HEAD_DIM = 64
N_HEADS = 16
N_KV_HEADS = 2
GROUP = 8
WINDOW = 128
EPS = 1e-6

F32 = jnp.float32
BF16 = jnp.bfloat16

LANES = 128
MXU_TILE = 256
SLABS = D_MODEL // LANES
KV_WIDTH = N_KV_HEADS * HEAD_DIM
PAIR_WIDTH = 2 * HEAD_DIM
REP_WIDTH = GROUP * HEAD_DIM
N_KEYS = WINDOW + CHUNK + N_META
HALO = 32
CONV_BASE = HALO - (CONV_WIDTH - 1)
DW_GROUP = 16

ROW_TILE = 512
CAST_GU_STEPS = 32
CAST_DN_STEPS = 16
FF_SPLITS = ((0, 1536), (1536, 2816))

VMEM_LIMIT = 56 * 1024 * 1024


def _params():
    return pltpu.CompilerParams(dimension_semantics=("arbitrary",), vmem_limit_bytes=VMEM_LIMIT)


def _const_spec(shape):
    nd = len(shape)
    return pl.BlockSpec(shape, lambda i: (0,) * nd, pipeline_mode=pl.Buffered(1))


def _layer_spec(shape, layer):
    return pl.BlockSpec((None,) + shape, lambda i: (layer, 0, 0), pipeline_mode=pl.Buffered(1))


def _rows_spec(rows, width, offset_blocks=0):
    return pl.BlockSpec((rows, width), lambda i: (i + offset_blocks, 0))


def _meta_spec(width):
    return pl.BlockSpec((N_META, width), lambda i: (0, 0))


def _rms(x, g):
    ms = jnp.mean(x * x, axis=-1, keepdims=True)
    return (x * lax.rsqrt(ms + EPS)) * g


def _dot(a, b):
    return jnp.dot(a, b, preferred_element_type=F32)


def _rms_split(x, g):
    r = lax.rsqrt(jnp.mean(x * x, axis=-1, keepdims=True) + EPS)
    return (x * g).astype(BF16), r


def _tile_and_meta(x_ref, xm_ref, body, out_refs, meta_refs):
    first = pl.program_id(0) == 0

    @pl.when(first)
    def _():
        outs = body(jnp.concatenate([x_ref[...], xm_ref[...]], axis=0))
        for val, o_ref, m_ref in zip(outs, out_refs, meta_refs):
            o_ref[...] = val[:ROW_TILE]
            if m_ref is not None:
                m_ref[...] = val[ROW_TILE:]

    @pl.when(jnp.logical_not(first))
    def _():
        for val, o_ref in zip(body(x_ref[...]), out_refs):
            o_ref[...] = val


def _swiglu_half(x, g_ref, wg_ref, wu_ref, wd_ref):
    h, r = _rms_split(x, g_ref[...])
    acc = None
    for c0, c1 in FF_SPLITS:
        gate = _dot(h, wg_ref[:, c0:c1]) * r
        up = _dot(h, wu_ref[:, c0:c1]) * r
        a = (gate * jax.nn.sigmoid(gate) * up).astype(BF16)
        part = _dot(a, wd_ref[c0:c1, :])
        acc = part if acc is None else acc + part
    return x + 0.5 * acc


def _cast_next(next_refs, out_refs):
    for n_ref, o_ref in zip(next_refs, out_refs):
        o_ref[...] = n_ref[...].astype(BF16)


def _ffn_kernel(x_ref, xm_ref, g_ref, wg_ref, wu_ref, wd_ref, ng_ref, nu_ref, nd_ref,
                o_ref, om_ref, og_ref, ou_ref, od_ref):
    _cast_next((ng_ref, nu_ref, nd_ref), (og_ref, ou_ref, od_ref))
    body = lambda x: (_swiglu_half(x, g_ref, wg_ref, wu_ref, wd_ref),)
    _tile_and_meta(x_ref, xm_ref, body, (o_ref,), (om_ref,))


def _ffn_first_kernel(xp_ref, xs_ref, xm_ref, g_ref, wg_ref, wu_ref, wd_ref, ng_ref, nu_ref, nd_ref,
                      o_ref, om_ref, og_ref, ou_ref, od_ref, xbuf, *, prompt_tiles):
    i = pl.program_id(0)
    _cast_next((ng_ref, nu_ref, nd_ref), (og_ref, ou_ref, od_ref))

    @pl.when(i < prompt_tiles)
    def _():
        xbuf[...] = xp_ref[...]

    @pl.when(i >= prompt_tiles)
    def _():
        xbuf[...] = xs_ref[...]

    body = lambda x: (_swiglu_half(x, g_ref, wg_ref, wu_ref, wd_ref),)
    _tile_and_meta(xbuf, xm_ref, body, (o_ref,), (om_ref,))


def _ffn_final_kernel(x_ref, xm_ref, g_ref, wg_ref, wu_ref, wd_ref, fg_ref, yp_ref, ys_ref, ybuf,
                      *, prompt_tiles):
    i = pl.program_id(0)
    body = lambda x: (_rms(_swiglu_half(x, g_ref, wg_ref, wu_ref, wd_ref), fg_ref[...]),)
    _tile_and_meta(x_ref, xm_ref, body, (ybuf,), (None,))

    @pl.when(i < prompt_tiles)
    def _():
        yp_ref[...] = ybuf[...]

    @pl.when(i >= prompt_tiles)
    def _():
        ys_ref[...] = ybuf[...]


def _ffn_weight_specs(layer):
    return [_layer_spec((1, D_MODEL), layer), _const_spec((D_MODEL, D_FF)),
            _const_spec((D_MODEL, D_FF)), _const_spec((D_FF, D_MODEL))]


def _next_weight_specs(layer, steps):
    assert steps >= max(CAST_GU_STEPS, CAST_DN_STEPS)
    gu_rows, dn_rows = D_MODEL // CAST_GU_STEPS, D_FF // CAST_DN_STEPS
    gu_i = lambda i: jnp.minimum(i, CAST_GU_STEPS - 1)
    dn_i = lambda i: jnp.minimum(i, CAST_DN_STEPS - 1)
    gu_in = pl.BlockSpec((None, gu_rows, D_FF), lambda i: (layer, gu_i(i), 0))
    dn_in = pl.BlockSpec((None, dn_rows, D_MODEL), lambda i: (layer, dn_i(i), 0))
    gu_out = pl.BlockSpec((gu_rows, D_FF), lambda i: (gu_i(i), 0))
    dn_out = pl.BlockSpec((dn_rows, D_MODEL), lambda i: (dn_i(i), 0))
    shapes = [jax.ShapeDtypeStruct((D_MODEL, D_FF), BF16), jax.ShapeDtypeStruct((D_MODEL, D_FF), BF16),
              jax.ShapeDtypeStruct((D_FF, D_MODEL), BF16)]
    return [gu_in, gu_in, dn_in], [gu_out, gu_out, dn_out], shapes


def _flat_out(n, width, dtype):
    return ([_rows_spec(ROW_TILE, width), _meta_spec(width)],
            [jax.ShapeDtypeStruct((n, width), dtype), jax.ShapeDtypeStruct((N_META, width), dtype)])


def _ffn(x, xm, g, w, layer, nxt, nxt_layer):
    n = x.shape[0]
    out_specs, out_shape = _flat_out(n, D_MODEL, F32)
    n_in, n_out, n_shape = _next_weight_specs(nxt_layer, n // ROW_TILE)
    x, xm, *w_next = pl.pallas_call(
        _ffn_kernel,
        grid=(n // ROW_TILE,),
        in_specs=[_rows_spec(ROW_TILE, D_MODEL), _meta_spec(D_MODEL)] + _ffn_weight_specs(layer) + n_in,
        out_specs=out_specs + n_out,
        out_shape=out_shape + n_shape,
        compiler_params=_params(),
        name="ffn",
    )(x, xm, g, *w, *nxt)
    return x, xm, w_next


def _ffn_first(xp, xs, xm, g, w, layer, nxt, nxt_layer, lay):
    pt, st = lay["n_prompt"] // ROW_TILE, lay["n_sample"] // ROW_TILE
    n = lay["n_prompt"] + lay["n_sample"]
    out_specs, out_shape = _flat_out(n, D_MODEL, F32)
    n_in, n_out, n_shape = _next_weight_specs(nxt_layer, n // ROW_TILE)
    x, xm, *w_next = pl.pallas_call(
        functools.partial(_ffn_first_kernel, prompt_tiles=pt),
        grid=(n // ROW_TILE,),
        in_specs=[pl.BlockSpec((ROW_TILE, D_MODEL), lambda i: (jnp.minimum(i, pt - 1), 0)),
                  pl.BlockSpec((ROW_TILE, D_MODEL), lambda i: (jnp.clip(i - pt, 0, st - 1), 0)),
                  _meta_spec(D_MODEL)] + _ffn_weight_specs(layer) + n_in,
        out_specs=out_specs + n_out,
        out_shape=out_shape + n_shape,
        scratch_shapes=[pltpu.VMEM((ROW_TILE, D_MODEL), F32)],
        compiler_params=_params(),
        name="ffn_first",
    )(xp, xs, xm, g, *w, *nxt)
    return x, xm, w_next


def _ffn_final(x, xm, g, w, fg, layer, lay):
    pt, st = lay["n_prompt"] // ROW_TILE, lay["n_sample"] // ROW_TILE
    n = x.shape[0]
    return pl.pallas_call(
        functools.partial(_ffn_final_kernel, prompt_tiles=pt),
        grid=(n // ROW_TILE,),
        in_specs=[_rows_spec(ROW_TILE, D_MODEL), _meta_spec(D_MODEL)] + _ffn_weight_specs(layer)
        + [_const_spec((1, D_MODEL))],
        out_specs=[pl.BlockSpec((ROW_TILE, D_MODEL), lambda i: (jnp.minimum(i, pt - 1), 0)),
                   pl.BlockSpec((ROW_TILE, D_MODEL), lambda i: (jnp.clip(i - pt, 0, st - 1), 0))],
        out_shape=[jax.ShapeDtypeStruct((lay["n_prompt"], D_MODEL), F32),
                   jax.ShapeDtypeStruct((lay["n_sample"], D_MODEL), F32)],
        scratch_shapes=[pltpu.VMEM((ROW_TILE, D_MODEL), F32)],
        compiler_params=_params(),
        name="ffn_final",
    )(x, xm, g, *w, fg)


def _glu_kernel(x_ref, xm_ref, g_ref, w_ref, b_ref, u_ref, um_ref):
    def body(x):
        h, r = _rms_split(x, g_ref[...])
        a = _dot(h, w_ref[...]) * r + b_ref[...]
        return (a[:, :D_MODEL] * jax.nn.sigmoid(a[:, D_MODEL:]),)

    _tile_and_meta(x_ref, xm_ref, body, (u_ref,), (um_ref,))


def _glu(x, xm, g, w, b, layer):
    n = x.shape[0]
    out_specs, out_shape = _flat_out(n, D_MODEL, F32)
    return pl.pallas_call(
        _glu_kernel,
        grid=(n // ROW_TILE,),
        in_specs=[_rows_spec(ROW_TILE, D_MODEL), _meta_spec(D_MODEL), _layer_spec((1, D_MODEL), layer),
                  _layer_spec((D_MODEL, 2 * D_MODEL), layer), _layer_spec((1, 2 * D_MODEL), layer)],
        out_specs=out_specs,
        out_shape=out_shape,
        compiler_params=_params(),
        name="glu",
    )(x, xm, g, w, b)


def _dwconv(ubuf, cbuf, wdw_ref, groups):
    for l in range(SLABS):
        lanes = slice(l * LANES, (l + 1) * LANES)
        for src, dst in groups:
            loads = [ubuf[l, pl.ds(src + j, 8, stride=2), :] for j in range(CONV_WIDTH + 1)]
            acc_e = acc_o = None
            for k in range(CONV_WIDTH):
                w = wdw_ref[k:k + 1, lanes]
                te, to = loads[k] * w, loads[k + 1] * w
                acc_e = te if acc_e is None else acc_e + te
                acc_o = to if acc_o is None else acc_o + to
            cbuf[l, pl.ds(dst, 8, stride=2), :] = acc_e
            cbuf[l, pl.ds(dst + 1, 8, stride=2), :] = acc_o


def _conv_tail(cbuf, x, bdw, lng, lnb, w2, b2):
    c = jnp.concatenate([cbuf[l] for l in range(SLABS)], axis=1) + bdw
    mu = jnp.mean(c, axis=-1, keepdims=True)
    cc = c - mu
    var = jnp.mean(cc * cc, axis=-1, keepdims=True)
    y = cc * lax.rsqrt(var + EPS) * lng + lnb
    y = (y * jax.nn.sigmoid(y)).astype(BF16)
    return x + (_dot(y, w2) + b2)


def _conv_prompt_kernel(u_ref, halo_ref, umeta_ref, x_ref, wdw_ref, bdw_ref, lng_ref, lnb_ref,
                        w2_ref, b2_ref, o_ref, ubuf, cbuf, *, tiles_per_stream):
    first = pl.program_id(0) % tiles_per_stream == 0

    @pl.when(first)
    def _():
        for l in range(SLABS):
            ubuf[l, 0:HALO - N_META, :] = jnp.zeros((HALO - N_META, LANES), F32)
            ubuf[l, HALO - N_META:HALO, :] = umeta_ref[:, l * LANES:(l + 1) * LANES]

    @pl.when(jnp.logical_not(first))
    def _():
        for l in range(SLABS):
            ubuf[l, 0:HALO, :] = halo_ref[:, l * LANES:(l + 1) * LANES]

    for l in range(SLABS):
        ubuf[l, HALO:HALO + ROW_TILE, :] = u_ref[:, l * LANES:(l + 1) * LANES]
    _dwconv(ubuf, cbuf, wdw_ref, [(r + CONV_BASE, r) for r in range(0, ROW_TILE, DW_GROUP)])
    o_ref[...] = _conv_tail(cbuf, x_ref[...], bdw_ref[...], lng_ref[...], lnb_ref[...],
                            w2_ref[...], b2_ref[...])


def _conv_sample_kernel(u_ref, st_ref, x_ref, wdw_ref, bdw_ref, lng_ref, lnb_ref,
                        w2_ref, b2_ref, o_ref, ubuf, cbuf, *, streams):
    span = HALO + CHUNK
    for s in range(streams):
        for l in range(SLABS):
            lanes = slice(l * LANES, (l + 1) * LANES)
            ubuf[l, s * span + CONV_BASE:s * span + HALO, :] = st_ref[s, :, lanes]
            ubuf[l, s * span + HALO:(s + 1) * span, :] = u_ref[s * CHUNK:(s + 1) * CHUNK, lanes]
    _dwconv(ubuf, cbuf, wdw_ref, [(s * span + r + CONV_BASE, s * CHUNK + r)
                                  for s in range(streams) for r in range(0, CHUNK, DW_GROUP)])
    o_ref[...] = _conv_tail(cbuf, x_ref[...], bdw_ref[...], lng_ref[...], lnb_ref[...],
                            w2_ref[...], b2_ref[...])


def _conv_meta_kernel(u_ref, x_ref, wdw_ref, bdw_ref, lng_ref, lnb_ref, w2_ref, b2_ref,
                      o_ref, ubuf, cbuf):
    for l in range(SLABS):
        ubuf[l, 0:HALO, :] = jnp.zeros((HALO, LANES), F32)
        ubuf[l, HALO:HALO + N_META, :] = u_ref[:, l * LANES:(l + 1) * LANES]
    _dwconv(ubuf, cbuf, wdw_ref, [(r + CONV_BASE, r) for r in range(0, N_META, DW_GROUP)])
    o_ref[...] = _conv_tail(cbuf, x_ref[...], bdw_ref[...], lng_ref[...], lnb_ref[...],
                            w2_ref[...], b2_ref[...])


def _conv_weight_specs(layer):
    return [_layer_spec((CONV_WIDTH, D_MODEL), layer), _layer_spec((1, D_MODEL), layer),
            _layer_spec((1, D_MODEL), layer), _layer_spec((1, D_MODEL), layer),
            _layer_spec((D_MODEL, D_MODEL), layer), _layer_spec((1, D_MODEL), layer)]


def _conv_layer(x, xm, u, um, state, cw, layer, lay):
    n = x.shape[0]
    out_shape = jax.ShapeDtypeStruct((n, D_MODEL), F32)
    tiles_per_stream = lay["seq"] // ROW_TILE
    n_prompt_tiles = lay["n_prompt"] // ROW_TILE
    slab = lambda rows: pltpu.VMEM((SLABS, rows, LANES), F32)

    x = pl.pallas_call(
        functools.partial(_conv_prompt_kernel, tiles_per_stream=tiles_per_stream),
        grid=(n_prompt_tiles,),
        in_specs=[_rows_spec(ROW_TILE, D_MODEL),
                  pl.BlockSpec((HALO, D_MODEL),
                               lambda i: (jnp.maximum(i * (ROW_TILE // HALO) - 1, 0), 0)),
                  _meta_spec(D_MODEL),
                  _rows_spec(ROW_TILE, D_MODEL)] + _conv_weight_specs(layer),
        out_specs=_rows_spec(ROW_TILE, D_MODEL),
        out_shape=out_shape,
        scratch_shapes=[slab(HALO + ROW_TILE), slab(ROW_TILE)],
        input_output_aliases={3: 0},
        compiler_params=_params(),
        name="conv_prompt",
    )(u, u, um, x, *cw)

    streams = ROW_TILE // CHUNK
    x = pl.pallas_call(
        functools.partial(_conv_sample_kernel, streams=streams),
        grid=(lay["n_sample"] // ROW_TILE,),
        in_specs=[_rows_spec(ROW_TILE, D_MODEL, n_prompt_tiles),
                  pl.BlockSpec((None, streams, CONV_WIDTH - 1, D_MODEL), lambda i: (layer, i, 0, 0)),
                  _rows_spec(ROW_TILE, D_MODEL, n_prompt_tiles)] + _conv_weight_specs(layer),
        out_specs=_rows_spec(ROW_TILE, D_MODEL, n_prompt_tiles),
        out_shape=out_shape,
        scratch_shapes=[slab(streams * (HALO + CHUNK)), slab(ROW_TILE)],
        input_output_aliases={2: 0},
        compiler_params=_params(),
        name="conv_sample",
    )(u, state, x, *cw)

    xm = pl.pallas_call(
        _conv_meta_kernel,
        grid=(1,),
        in_specs=[_meta_spec(D_MODEL), _meta_spec(D_MODEL)] + _conv_weight_specs(layer),
        out_specs=_meta_spec(D_MODEL),
        out_shape=jax.ShapeDtypeStruct((N_META, D_MODEL), F32),
        scratch_shapes=[slab(HALO + N_META), slab(N_META)],
        compiler_params=_params(),
        name="conv_meta",
    )(um, xm, *cw)
    return x, xm


def _split_dot(a, b):
    hi = a.astype(BF16)
    lo = (a - hi.astype(F32)).astype(BF16)
    return _dot(hi, b) + _dot(lo, b)


def _head_mean(sq, b_ref):
    w = b_ref.shape[0]
    parts = [_split_dot(sq[:, j * w:(j + 1) * w], b_ref[...]) for j in range(sq.shape[1] // w)]
    return parts[0] if len(parts) == 1 else jnp.concatenate(parts, axis=1)


def _qkv_kernel(x_ref, xm_ref, g_ref, w_ref, bq_ref, bk_ref, e_ref, gq_ref, gk_ref,
                q_ref, kp_ref, vp_ref, k_ref, v_ref, qm_ref, kpm_ref, vpm_ref, km_ref, vm_ref):
    def body(x):
        h, r = _rms_split(x, g_ref[...])
        qkv = _dot(h, w_ref[...]) * r
        nq = N_HEADS * HEAD_DIM
        q = qkv[:, :nq]
        k = qkv[:, nq:nq + KV_WIDTH]
        v = qkv[:, nq + KV_WIDTH:]
        q = (q * lax.rsqrt(_head_mean(q * q, bq_ref) + EPS)) * gq_ref[...]
        k = (k * lax.rsqrt(_head_mean(k * k, bk_ref) + EPS)) * gk_ref[...]
        return ((q * (HEAD_DIM ** -0.5)).astype(BF16),
                _dot(k.astype(BF16), e_ref[...]).astype(BF16),
                _dot(v.astype(BF16), e_ref[...]).astype(BF16), k, v)

    _tile_and_meta(x_ref, xm_ref, body, (q_ref, kp_ref, vp_ref, k_ref, v_ref),
                   (qm_ref, kpm_ref, vpm_ref, km_ref, vm_ref))


def _qkv(x, xm, g, w, bq, bk, e, gq, gk, layer):
    n = x.shape[0]
    qkv_w = (N_HEADS + 2 * N_KV_HEADS) * HEAD_DIM
    wide = N_KV_HEADS * PAIR_WIDTH
    outs = [(D_MODEL, BF16), (wide, BF16), (wide, BF16), (KV_WIDTH, F32), (KV_WIDTH, F32)]
    return pl.pallas_call(
        _qkv_kernel,
        grid=(n // ROW_TILE,),
        in_specs=[_rows_spec(ROW_TILE, D_MODEL), _meta_spec(D_MODEL), _layer_spec((1, D_MODEL), layer),
                  _layer_spec((D_MODEL, qkv_w), layer), _const_spec((MXU_TILE, MXU_TILE)),
                  _const_spec((KV_WIDTH, KV_WIDTH)), _const_spec((KV_WIDTH, wide)),
                  _const_spec((1, D_MODEL)), _const_spec((1, KV_WIDTH))],
        out_specs=[_rows_spec(ROW_TILE, w_) for w_, _ in outs] + [_meta_spec(w_) for w_, _ in outs],
        out_shape=[jax.ShapeDtypeStruct((n, w_), d_) for w_, d_ in outs]
        + [jax.ShapeDtypeStruct((N_META, w_), d_) for w_, d_ in outs],
        compiler_params=_params(),
        name="qkv",
    )(x, xm, g, w, bq, bk, e, gq, gk)


def _attend(qt, kx, vx, sink_row, bias_col):
    nq = qt.shape[0]
    lane_half = lax.broadcasted_iota(jnp.int32, (nq, PAIR_WIDTH), 1) // HEAD_DIM
    zero = jnp.zeros((nq, PAIR_WIDTH), qt.dtype)
    blocks = []
    for h in range(N_HEADS):
        kh, g = divmod(h, GROUP)
        lo = kh * REP_WIDTH + (g // 2) * PAIR_WIDTH
        own = jnp.where(lane_half == g % 2, qt[:, lo:lo + PAIR_WIDTH], jnp.zeros((), qt.dtype))
        blocks.append(jnp.concatenate([own if c == kh else zero for c in range(N_KV_HEADS)], axis=1))
    ql = jnp.concatenate(blocks, axis=0)
    st = lax.dot_general(kx, ql, (((1,), (1,)), ((), ())), preferred_element_type=F32)
    if bias_col is not None:
        st = st + bias_col
    m = jnp.maximum(jnp.max(st, axis=0, keepdims=True), sink_row)
    p = jnp.exp(st - m)
    den = jnp.sum(p, axis=0, keepdims=True) + jnp.exp(sink_row - m)
    pn = (p * (1.0 / den)).astype(BF16)
    o2 = _dot(pn.T, vx)
    tiles = []
    for h in range(0, N_HEADS, 2):
        lanes = slice((h // GROUP) * PAIR_WIDTH, (h // GROUP + 1) * PAIR_WIDTH)
        tiles.append(jnp.where(lane_half == 0, o2[h * nq:(h + 1) * nq, lanes],
                               o2[(h + 1) * nq:(h + 2) * nq, lanes]))
    return jnp.concatenate(tiles, axis=1)


def _attend_small(qg, kx, vx, sink_col):
    nq = qg.shape[0]
    rows = GROUP * nq
    row_head = lax.broadcasted_iota(jnp.int32, (rows, REP_WIDTH), 0) // nq
    lane_head = lax.broadcasted_iota(jnp.int32, (rows, REP_WIDTH), 1) // HEAD_DIM
    qs = jnp.where(row_head == lane_head, jnp.concatenate([qg] * GROUP, axis=0),
                   jnp.zeros((), qg.dtype))
    s = lax.dot_general(qs, kx, (((1,), (1,)), ((), ())), preferred_element_type=F32)
    m = jnp.maximum(jnp.max(s, axis=-1, keepdims=True), sink_col)
    p = jnp.exp(s - m)
    den = jnp.sum(p, axis=-1, keepdims=True) + jnp.exp(sink_col - m)
    pn = (p * (1.0 / den)).astype(BF16)
    ow = _dot(pn, vx)
    out_head = lax.broadcasted_iota(jnp.int32, (nq, REP_WIDTH), 1) // HEAD_DIM
    o = None
    for g in range(GROUP):
        t = jnp.where(out_head == g, ow[g * nq:(g + 1) * nq, :], 0.0)
        o = t if o is None else o + t
    return o


def _attn_prompt_kernel(q_ref, k_ref, khalo_ref, kmeta_ref, v_ref, vhalo_ref, vmeta_ref,
                        x_ref, sink_ref, wo_ref, o_ref, obuf, *, tiles_per_stream):
    chunks = ROW_TILE // CHUNK
    first_chunk = (pl.program_id(0) % tiles_per_stream) * chunks
    key_pos = lax.broadcasted_iota(jnp.int32, (N_KEYS, 1), 0)

    def band(main_ref, halo_ref, meta_ref, c):
        parts = []
        for back in (2, 1, 0):
            j = c - back
            if j < 0:
                parts.append(halo_ref[(2 + j) * CHUNK:(3 + j) * CHUNK, :])
            else:
                parts.append(main_ref[j * CHUNK:(j + 1) * CHUNK, :])
        parts.append(meta_ref[...])
        return jnp.concatenate(parts, axis=0)

    for c in range(chunks):
        bias = None
        if c < WINDOW // CHUNK:
            missing = jnp.clip(WINDOW // CHUNK - (first_chunk + c), 0, WINDOW // CHUNK) * CHUNK
            bias = jnp.where(key_pos < missing, -jnp.inf, 0.0).astype(F32)
        rows = slice(c * CHUNK, (c + 1) * CHUNK)
        o = _attend(q_ref[rows, :], band(k_ref, khalo_ref, kmeta_ref, c),
                    band(v_ref, vhalo_ref, vmeta_ref, c), sink_ref[...], bias)
        obuf[rows, :] = o.astype(BF16)
    o_ref[...] = x_ref[...] + _dot(obuf[...], wo_ref[...])


def _attn_sample_kernel(q_ref, k_ref, ck_ref, mk_ref, v_ref, cv_ref, mv_ref,
                        x_ref, sink_ref, wo_ref, o_ref, obuf, *, streams):
    for s in range(streams):
        rows = slice(s * CHUNK, (s + 1) * CHUNK)
        kx = jnp.concatenate([ck_ref[s], k_ref[rows, :], mk_ref[s]], axis=0)
        vx = jnp.concatenate([cv_ref[s], v_ref[rows, :], mv_ref[s]], axis=0)
        obuf[rows, :] = _attend(q_ref[rows, :], kx, vx, sink_ref[...], None).astype(BF16)
    o_ref[...] = x_ref[...] + _dot(obuf[...], wo_ref[...])


def _attn_meta_kernel(q_ref, k_ref, v_ref, x_ref, sink_ref, wo_ref, o_ref, obuf):
    for kh in range(N_KV_HEADS):
        lanes = slice(kh * PAIR_WIDTH, (kh + 1) * PAIR_WIDTH)
        qlanes = slice(kh * REP_WIDTH, (kh + 1) * REP_WIDTH)
        kx = jnp.concatenate([k_ref[:, lanes]] * (GROUP // 2), axis=1)
        vx = jnp.concatenate([v_ref[:, lanes]] * (GROUP // 2), axis=1)
        o = _attend_small(q_ref[:, qlanes], kx, vx,
                          sink_ref[kh * GROUP * N_META:(kh + 1) * GROUP * N_META, :])
        obuf[:, qlanes] = o.astype(BF16)
    o_ref[...] = x_ref[...] + _dot(obuf[...], wo_ref[...])


def _attn_layer(x, xm, q, kp, vp, qm, kpm, vpm, ck, cv, mk, mv, sinks, wo, layer, lay):
    n = x.shape[0]
    out_shape = jax.ShapeDtypeStruct((n, D_MODEL), F32)
    wide = N_KV_HEADS * PAIR_WIDTH
    tiles_per_stream = lay["seq"] // ROW_TILE
    n_prompt_tiles = lay["n_prompt"] // ROW_TILE
    sinks = sinks.astype(F32)
    sink_rows = jnp.repeat(sinks, CHUNK).reshape(1, N_HEADS * CHUNK)
    sink_meta = jnp.repeat(sinks, N_META)[:, None]
    wo_spec = _layer_spec((D_MODEL, D_MODEL), layer)
    sink_spec = _const_spec((1, N_HEADS * CHUNK))

    main = _rows_spec(ROW_TILE, wide)
    halo = pl.BlockSpec((WINDOW, wide), lambda i: (jnp.maximum(i * (ROW_TILE // WINDOW) - 1, 0), 0))
    x = pl.pallas_call(
        functools.partial(_attn_prompt_kernel, tiles_per_stream=tiles_per_stream),
        grid=(n_prompt_tiles,),
        in_specs=[_rows_spec(ROW_TILE, D_MODEL), main, halo, _meta_spec(wide), main, halo, _meta_spec(wide),
                  _rows_spec(ROW_TILE, D_MODEL), sink_spec, wo_spec],
        out_specs=_rows_spec(ROW_TILE, D_MODEL),
        out_shape=out_shape,
        scratch_shapes=[pltpu.VMEM((ROW_TILE, D_MODEL), BF16)],
        input_output_aliases={7: 0},
        compiler_params=_params(),
        name="attn_prompt",
    )(q, kp, kp, kpm, vp, vp, vpm, x, sink_rows, wo)

    streams = ROW_TILE // CHUNK
    smain = _rows_spec(ROW_TILE, wide, n_prompt_tiles)
    cache = pl.BlockSpec((None, streams, WINDOW, wide), lambda i: (layer, i, 0, 0))
    mcache = pl.BlockSpec((None, streams, N_META, wide), lambda i: (layer, i, 0, 0))
    x = pl.pallas_call(
        functools.partial(_attn_sample_kernel, streams=streams),
        grid=(lay["n_sample"] // ROW_TILE,),
        in_specs=[_rows_spec(ROW_TILE, D_MODEL, n_prompt_tiles), smain, cache, mcache, smain, cache, mcache,
                  _rows_spec(ROW_TILE, D_MODEL, n_prompt_tiles), sink_spec, wo_spec],
        out_specs=_rows_spec(ROW_TILE, D_MODEL, n_prompt_tiles),
        out_shape=out_shape,
        scratch_shapes=[pltpu.VMEM((ROW_TILE, D_MODEL), BF16)],
        input_output_aliases={7: 0},
        compiler_params=_params(),
        name="attn_sample",
    )(q, kp, ck, mk, vp, cv, mv, x, sink_rows, wo)

    xm = pl.pallas_call(
        _attn_meta_kernel,
        grid=(1,),
        in_specs=[_meta_spec(D_MODEL), _meta_spec(wide), _meta_spec(wide), _meta_spec(D_MODEL),
                  _const_spec((N_HEADS * N_META, 1)), wo_spec],
        out_specs=_meta_spec(D_MODEL),
        out_shape=jax.ShapeDtypeStruct((N_META, D_MODEL), F32),
        scratch_shapes=[pltpu.VMEM((N_META, D_MODEL), BF16)],
        compiler_params=_params(),
        name="attn_meta",
    )(qm, kpm, vpm, xm, sink_meta, wo)
    return x, xm


def _head_mean_matrix(width):
    idx = np.arange(width) // HEAD_DIM
    return jnp.asarray((idx[:, None] == idx[None, :]).astype(np.float32) / HEAD_DIM, dtype=BF16)


def _pair_matrix():
    src = np.arange(KV_WIDTH)
    dst = np.arange(N_KV_HEADS * PAIR_WIDTH)
    same_head = (src[:, None] // HEAD_DIM) == (dst[None, :] // PAIR_WIDTH)
    same_dim = (src[:, None] % HEAD_DIM) == (dst[None, :] % HEAD_DIM)
    return jnp.asarray((same_head & same_dim).astype(np.float32), dtype=BF16)


def _pair_heads(c):
    c = c.astype(BF16)
    c = jnp.broadcast_to(c[..., :, None, :], c.shape[:-1] + (2, HEAD_DIM))
    return c.reshape(c.shape[:-3] + (N_KV_HEADS * PAIR_WIDTH,))


def kernel(x_prompt, x_sample, cache_swa_k, cache_swa_v, cache_meta_k, cache_meta_v, state_conv, meta_tokens, ffn1_norm, ffn1_w_gate, ffn1_w_up, ffn1_w_down, ffn2_norm, ffn2_w_gate, ffn2_w_up, ffn2_w_down, conv_norm, conv_w_pw1, conv_b_pw1, conv_w_dw, conv_b_dw, conv_ln_g, conv_ln_b, conv_w_pw2, conv_b_pw2, attn_norm, attn_w_qkv, attn_q_gain, attn_k_gain, attn_sinks, attn_w_o, final_norm):
    batch, seq, _ = x_prompt.shape
    dec_batch, dec_seq, _ = x_sample.shape
    assert seq % ROW_TILE == 0 and dec_seq == CHUNK and (dec_batch * dec_seq) % ROW_TILE == 0
    n_prompt = batch * seq
    n_sample = dec_batch * dec_seq
    n_flat = n_prompt + n_sample
    lay = dict(seq=seq, n_prompt=n_prompt, n_sample=n_sample)

    rows3 = lambda a: a.astype(F32)[:, None, :]
    ffn_f32 = ((ffn1_w_gate, ffn1_w_up, ffn1_w_down), (ffn2_w_gate, ffn2_w_up, ffn2_w_down))
    ffn_g = (rows3(ffn1_norm), rows3(ffn2_norm))
    w_cur = [a[0].astype(BF16) for a in ffn_f32[0]]
    conv_w = (conv_w_dw.astype(F32), rows3(conv_b_dw), rows3(conv_ln_g), rows3(conv_ln_b),
              conv_w_pw2.astype(BF16), rows3(conv_b_pw2))
    glu_w = (rows3(conv_norm), conv_w_pw1.astype(BF16), rows3(conv_b_pw1))
    attn_g, w_qkv, w_o = rows3(attn_norm), attn_w_qkv.astype(BF16), attn_w_o.astype(BF16)
    state = state_conv.astype(F32)
    ck, cv = _pair_heads(cache_swa_k), _pair_heads(cache_swa_v)
    mk, mv = _pair_heads(cache_meta_k), _pair_heads(cache_meta_v)
    bq = _head_mean_matrix(MXU_TILE)
    bk = _head_mean_matrix(KV_WIDTH)
    pair = _pair_matrix()
    row = lambda a: a.reshape(1, -1).astype(F32)
    heads = lambda a: a.reshape(a.shape[:-1] + (N_KV_HEADS, HEAD_DIM))
    tail = CONV_WIDTH - 1

    conv_p, conv_s = [], []
    swk_p, swv_p, mk_p, mv_p, swk_s, swv_s = [], [], [], [], [], []
    x = xm = None
    for i in range(DEPTH):
        if i == 0:
            x, xm, w_cur = _ffn_first(x_prompt.reshape(n_prompt, D_MODEL).astype(F32),
                                      x_sample.reshape(n_sample, D_MODEL).astype(F32),
                                      meta_tokens.astype(F32), ffn_g[0], w_cur, i, ffn_f32[1], i, lay)
        else:
            x, xm, w_cur = _ffn(x, xm, ffn_g[0], w_cur, i, ffn_f32[1], i)
        j = i // 2
        if i % 2 == 0:
            u, um = _glu(x, xm, *glu_w, j)
            x, xm = _conv_layer(x, xm, u, um, state, conv_w, j, lay)
            conv_p.append(jnp.stack([u[(b + 1) * seq - tail:(b + 1) * seq] for b in range(batch)]))
            conv_s.append(u[n_prompt:n_flat].reshape(dec_batch, dec_seq, D_MODEL)[:, dec_seq - tail:])
        else:
            q, kp, vp, k, v, qm, kpm, vpm, km, vm = _qkv(
                x, xm, attn_g, w_qkv, bq, bk, pair, row(jnp.tile(attn_q_gain[j], N_HEADS)),
                row(jnp.tile(attn_k_gain[j], N_KV_HEADS)), j)
            x, xm = _attn_layer(x, xm, q, kp, vp, qm, kpm, vpm, ck, cv, mk, mv,
                                attn_sinks[j], w_o, j, lay)
            swk_p.append(jnp.stack([heads(k[(b + 1) * seq - WINDOW:(b + 1) * seq]) for b in range(batch)]))
            swv_p.append(jnp.stack([heads(v[(b + 1) * seq - WINDOW:(b + 1) * seq]) for b in range(batch)]))
            mk_p.append(jnp.broadcast_to(heads(km)[None], (batch, N_META, N_KV_HEADS, HEAD_DIM)))
            mv_p.append(jnp.broadcast_to(heads(vm)[None], (batch, N_META, N_KV_HEADS, HEAD_DIM)))
            ks = heads(k[n_prompt:n_flat].reshape(dec_batch, dec_seq, KV_WIDTH))
            vs = heads(v[n_prompt:n_flat].reshape(dec_batch, dec_seq, KV_WIDTH))
            swk_s.append(jnp.concatenate([cache_swa_k[j].astype(F32)[:, dec_seq:], ks], axis=1))
            swv_s.append(jnp.concatenate([cache_swa_v[j].astype(F32)[:, dec_seq:], vs], axis=1))
        if i < DEPTH - 1:
            x, xm, w_cur = _ffn(x, xm, ffn_g[1], w_cur, i, ffn_f32[0], i + 1)
    y_prompt, y_sample = _ffn_final(x, xm, ffn_g[1], w_cur, row(final_norm), DEPTH - 1, lay)

    return (y_prompt.reshape(batch, seq, D_MODEL), y_sample.reshape(dec_batch, dec_seq, D_MODEL),
            jnp.stack(swk_p), jnp.stack(swv_p), jnp.stack(mk_p), jnp.stack(mv_p), jnp.stack(conv_p),
            jnp.stack(swk_s), jnp.stack(swv_s), jnp.stack(conv_s))
```

```python
import functools

import numpy as np
import jax
import jax.numpy as jnp
from jax import lax
from jax.experimental import pallas as pl
from jax.experimental.pallas import tpu as pltpu

D_MODEL = 1024
D_FF = 2816
DEPTH = 4
N_META = 16
CHUNK = 64
CONV_WIDTH = 31
HEAD_DIM = 64
N_HEADS = 16
N_KV_HEADS = 2
GROUP = 8
WINDOW = 128
EPS = 1e-6

F32 = jnp.float32
BF16 = jnp.bfloat16

LANES = 128
MXU_TILE = 256
SLABS = D_MODEL // LANES
KV_WIDTH = N_KV_HEADS * HEAD_DIM
PAIR_WIDTH = 2 * HEAD_DIM
REP_WIDTH = GROUP * HEAD_DIM
N_KEYS = WINDOW + CHUNK + N_META
HALO = 32
CONV_BASE = HALO - (CONV_WIDTH - 1)
DW_GROUP = 16

ROW_TILE = 512
ATTN_BATCH = 4
CAST_GU_STEPS = 32
CAST_DN_STEPS = 16
FF_SPLITS = ((0, 1536), (1536, 2816))

VMEM_LIMIT = 56 * 1024 * 1024


def _params():
    return pltpu.CompilerParams(dimension_semantics=("arbitrary",), vmem_limit_bytes=VMEM_LIMIT)


def _const_spec(shape):
    nd = len(shape)
    return pl.BlockSpec(shape, lambda i: (0,) * nd, pipeline_mode=pl.Buffered(1))


def _layer_spec(shape, layer):
    return pl.BlockSpec((None,) + shape, lambda i: (layer, 0, 0), pipeline_mode=pl.Buffered(1))


def _rows_spec(rows, width, offset_blocks=0):
    return pl.BlockSpec((rows, width), lambda i: (i + offset_blocks, 0))


def _meta_spec(width):
    return pl.BlockSpec((N_META, width), lambda i: (0, 0))


def _rms(x, g):
    ms = jnp.mean(x * x, axis=-1, keepdims=True)
    return (x * lax.rsqrt(ms + EPS)) * g


def _dot(a, b):
    return jnp.dot(a, b, preferred_element_type=F32)


def _rms_split(x, g):
    r = lax.rsqrt(jnp.mean(x * x, axis=-1, keepdims=True) + EPS)
    return (x * g).astype(BF16), r


def _tile_and_meta(x_ref, xm_ref, body, out_refs, meta_refs):
    first = pl.program_id(0) == 0

    @pl.when(first)
    def _():
        outs = body(jnp.concatenate([x_ref[...], xm_ref[...]], axis=0))
        for val, o_ref, m_ref in zip(outs, out_refs, meta_refs):
            o_ref[...] = val[:ROW_TILE]
            if m_ref is not None:
                m_ref[...] = val[ROW_TILE:]

    @pl.when(jnp.logical_not(first))
    def _():
        for val, o_ref in zip(body(x_ref[...]), out_refs):
            o_ref[...] = val


def _swiglu_half(x, g_ref, wg_ref, wu_ref, wd_ref):
    h, r = _rms_split(x, g_ref[...])
    acc = None
    for c0, c1 in FF_SPLITS:
        gate = _dot(h, wg_ref[:, c0:c1]) * r
        up = _dot(h, wu_ref[:, c0:c1]) * r
        a = (gate * jax.nn.sigmoid(gate) * up).astype(BF16)
        part = _dot(a, wd_ref[c0:c1, :])
        acc = part if acc is None else acc + part
    return x + 0.5 * acc


def _cast_next(next_refs, out_refs):
    for n_ref, o_ref in zip(next_refs, out_refs):
        o_ref[...] = n_ref[...].astype(BF16)


def _ffn_kernel(x_ref, xm_ref, g_ref, wg_ref, wu_ref, wd_ref, ng_ref, nu_ref, nd_ref,
                o_ref, om_ref, og_ref, ou_ref, od_ref):
    _cast_next((ng_ref, nu_ref, nd_ref), (og_ref, ou_ref, od_ref))
    body = lambda x: (_swiglu_half(x, g_ref, wg_ref, wu_ref, wd_ref),)
    _tile_and_meta(x_ref, xm_ref, body, (o_ref,), (om_ref,))


def _ffn_first_kernel(xp_ref, xs_ref, xm_ref, g_ref, wg_ref, wu_ref, wd_ref, ng_ref, nu_ref, nd_ref,
                      o_ref, om_ref, og_ref, ou_ref, od_ref, xbuf, *, prompt_tiles):
    i = pl.program_id(0)
    _cast_next((ng_ref, nu_ref, nd_ref), (og_ref, ou_ref, od_ref))

    @pl.when(i < prompt_tiles)
    def _():
        xbuf[...] = xp_ref[...]

    @pl.when(i >= prompt_tiles)
    def _():
        xbuf[...] = xs_ref[...]

    body = lambda x: (_swiglu_half(x, g_ref, wg_ref, wu_ref, wd_ref),)
    _tile_and_meta(xbuf, xm_ref, body, (o_ref,), (om_ref,))


def _ffn_final_kernel(x_ref, xm_ref, g_ref, wg_ref, wu_ref, wd_ref, fg_ref, yp_ref, ys_ref, ybuf,
                      *, prompt_tiles):
    i = pl.program_id(0)
    body = lambda x: (_rms(_swiglu_half(x, g_ref, wg_ref, wu_ref, wd_ref), fg_ref[...]),)
    _tile_and_meta(x_ref, xm_ref, body, (ybuf,), (None,))

    @pl.when(i < prompt_tiles)
    def _():
        yp_ref[...] = ybuf[...]

    @pl.when(i >= prompt_tiles)
    def _():
        ys_ref[...] = ybuf[...]


def _ffn_weight_specs(layer):
    return [_layer_spec((1, D_MODEL), layer), _const_spec((D_MODEL, D_FF)),
            _const_spec((D_MODEL, D_FF)), _const_spec((D_FF, D_MODEL))]


def _next_weight_specs(layer, steps):
    assert steps >= max(CAST_GU_STEPS, CAST_DN_STEPS)
    gu_rows, dn_rows = D_MODEL // CAST_GU_STEPS, D_FF // CAST_DN_STEPS
    gu_i = lambda i: jnp.minimum(i, CAST_GU_STEPS - 1)
    dn_i = lambda i: jnp.minimum(i, CAST_DN_STEPS - 1)
    gu_in = pl.BlockSpec((None, gu_rows, D_FF), lambda i: (layer, gu_i(i), 0))
    dn_in = pl.BlockSpec((None, dn_rows, D_MODEL), lambda i: (layer, dn_i(i), 0))
    gu_out = pl.BlockSpec((gu_rows, D_FF), lambda i: (gu_i(i), 0))
    dn_out = pl.BlockSpec((dn_rows, D_MODEL), lambda i: (dn_i(i), 0))
    shapes = [jax.ShapeDtypeStruct((D_MODEL, D_FF), BF16), jax.ShapeDtypeStruct((D_MODEL, D_FF), BF16),
              jax.ShapeDtypeStruct((D_FF, D_MODEL), BF16)]
    return [gu_in, gu_in, dn_in], [gu_out, gu_out, dn_out], shapes


def _flat_out(n, width, dtype):
    return ([_rows_spec(ROW_TILE, width), _meta_spec(width)],
            [jax.ShapeDtypeStruct((n, width), dtype), jax.ShapeDtypeStruct((N_META, width), dtype)])


def _ffn(x, xm, g, w, layer, nxt, nxt_layer):
    n = x.shape[0]
    out_specs, out_shape = _flat_out(n, D_MODEL, F32)
    n_in, n_out, n_shape = _next_weight_specs(nxt_layer, n // ROW_TILE)
    x, xm, *w_next = pl.pallas_call(
        _ffn_kernel,
        grid=(n // ROW_TILE,),
        in_specs=[_rows_spec(ROW_TILE, D_MODEL), _meta_spec(D_MODEL)] + _ffn_weight_specs(layer) + n_in,
        out_specs=out_specs + n_out,
        out_shape=out_shape + n_shape,
        compiler_params=_params(),
        name="ffn",
    )(x, xm, g, *w, *nxt)
    return x, xm, w_next


def _ffn_first(xp, xs, xm, g, w, layer, nxt, nxt_layer, lay):
    pt, st = lay["n_prompt"] // ROW_TILE, lay["n_sample"] // ROW_TILE
    n = lay["n_prompt"] + lay["n_sample"]
    out_specs, out_shape = _flat_out(n, D_MODEL, F32)
    n_in, n_out, n_shape = _next_weight_specs(nxt_layer, n // ROW_TILE)
    x, xm, *w_next = pl.pallas_call(
        functools.partial(_ffn_first_kernel, prompt_tiles=pt),
        grid=(n // ROW_TILE,),
        in_specs=[pl.BlockSpec((ROW_TILE, D_MODEL), lambda i: (jnp.minimum(i, pt - 1), 0)),
                  pl.BlockSpec((ROW_TILE, D_MODEL), lambda i: (jnp.clip(i - pt, 0, st - 1), 0)),
                  _meta_spec(D_MODEL)] + _ffn_weight_specs(layer) + n_in,
        out_specs=out_specs + n_out,
        out_shape=out_shape + n_shape,
        scratch_shapes=[pltpu.VMEM((ROW_TILE, D_MODEL), F32)],
        compiler_params=_params(),
        name="ffn_first",
    )(xp, xs, xm, g, *w, *nxt)
    return x, xm, w_next


def _ffn_final(x, xm, g, w, fg, layer, lay):
    pt, st = lay["n_prompt"] // ROW_TILE, lay["n_sample"] // ROW_TILE
    n = x.shape[0]
    return pl.pallas_call(
        functools.partial(_ffn_final_kernel, prompt_tiles=pt),
        grid=(n // ROW_TILE,),
        in_specs=[_rows_spec(ROW_TILE, D_MODEL), _meta_spec(D_MODEL)] + _ffn_weight_specs(layer)
        + [_const_spec((1, D_MODEL))],
        out_specs=[pl.BlockSpec((ROW_TILE, D_MODEL), lambda i: (jnp.minimum(i, pt - 1), 0)),
                   pl.BlockSpec((ROW_TILE, D_MODEL), lambda i: (jnp.clip(i - pt, 0, st - 1), 0))],
        out_shape=[jax.ShapeDtypeStruct((lay["n_prompt"], D_MODEL), F32),
                   jax.ShapeDtypeStruct((lay["n_sample"], D_MODEL), F32)],
        scratch_shapes=[pltpu.VMEM((ROW_TILE, D_MODEL), F32)],
        compiler_params=_params(),
        name="ffn_final",
    )(x, xm, g, *w, fg)


def _glu_kernel(x_ref, xm_ref, g_ref, w_ref, b_ref, u_ref, um_ref):
    def body(x):
        h, r = _rms_split(x, g_ref[...])
        a = _dot(h, w_ref[...]) * r + b_ref[...]
        return (a[:, :D_MODEL] * jax.nn.sigmoid(a[:, D_MODEL:]),)

    _tile_and_meta(x_ref, xm_ref, body, (u_ref,), (um_ref,))


def _glu(x, xm, g, w, b, layer):
    n = x.shape[0]
    out_specs, out_shape = _flat_out(n, D_MODEL, F32)
    return pl.pallas_call(
        _glu_kernel,
        grid=(n // ROW_TILE,),
        in_specs=[_rows_spec(ROW_TILE, D_MODEL), _meta_spec(D_MODEL), _layer_spec((1, D_MODEL), layer),
                  _layer_spec((D_MODEL, 2 * D_MODEL), layer), _layer_spec((1, 2 * D_MODEL), layer)],
        out_specs=out_specs,
        out_shape=out_shape,
        compiler_params=_params(),
        name="glu",
    )(x, xm, g, w, b)


def _dwconv(ubuf, cbuf, wdw_ref, groups):
    for l in range(SLABS):
        lanes = slice(l * LANES, (l + 1) * LANES)
        for src, dst in groups:
            loads = [ubuf[l, pl.ds(src + j, 8, stride=2), :] for j in range(CONV_WIDTH + 1)]
            acc_e = acc_o = None
            for k in range(CONV_WIDTH):
                w = wdw_ref[k:k + 1, lanes]
                te, to = loads[k] * w, loads[k + 1] * w
                acc_e = te if acc_e is None else acc_e + te
                acc_o = to if acc_o is None else acc_o + to
            cbuf[l, pl.ds(dst, 8, stride=2), :] = acc_e
            cbuf[l, pl.ds(dst + 1, 8, stride=2), :] = acc_o


def _conv_tail(cbuf, x, bdw, lng, lnb, w2, b2):
    c = jnp.concatenate([cbuf[l] for l in range(SLABS)], axis=1) + bdw
    mu = jnp.mean(c, axis=-1, keepdims=True)
    cc = c - mu
    var = jnp.mean(cc * cc, axis=-1, keepdims=True)
    y = cc * lax.rsqrt(var + EPS) * lng + lnb
    y = (y * jax.nn.sigmoid(y)).astype(BF16)
    return x + (_dot(y, w2) + b2)


def _conv_prompt_kernel(u_ref, halo_ref, umeta_ref, x_ref, wdw_ref, bdw_ref, lng_ref, lnb_ref,
                        w2_ref, b2_ref, o_ref, ubuf, cbuf, *, tiles_per_stream):
    first = pl.program_id(0) % tiles_per_stream == 0

    @pl.when(first)
    def _():
        for l in range(SLABS):
            ubuf[l, 0:HALO - N_META, :] = jnp.zeros((HALO - N_META, LANES), F32)
            ubuf[l, HALO - N_META:HALO, :] = umeta_ref[:, l * LANES:(l + 1) * LANES]

    @pl.when(jnp.logical_not(first))
    def _():
        for l in range(SLABS):
            ubuf[l, 0:HALO, :] = halo_ref[:, l * LANES:(l + 1) * LANES]

    for l in range(SLABS):
        ubuf[l, HALO:HALO + ROW_TILE, :] = u_ref[:, l * LANES:(l + 1) * LANES]
    _dwconv(ubuf, cbuf, wdw_ref, [(r + CONV_BASE, r) for r in range(0, ROW_TILE, DW_GROUP)])
    o_ref[...] = _conv_tail(cbuf, x_ref[...], bdw_ref[...], lng_ref[...], lnb_ref[...],
                            w2_ref[...], b2_ref[...])


def _conv_sample_kernel(u_ref, st_ref, x_ref, wdw_ref, bdw_ref, lng_ref, lnb_ref,
                        w2_ref, b2_ref, o_ref, ubuf, cbuf, *, streams):
    span = HALO + CHUNK
    for s in range(streams):
        for l in range(SLABS):
            lanes = slice(l * LANES, (l + 1) * LANES)
            ubuf[l, s * span + CONV_BASE:s * span + HALO, :] = st_ref[s, :, lanes]
            ubuf[l, s * span + HALO:(s + 1) * span, :] = u_ref[s * CHUNK:(s + 1) * CHUNK, lanes]
    _dwconv(ubuf, cbuf, wdw_ref, [(s * span + r + CONV_BASE, s * CHUNK + r)
                                  for s in range(streams) for r in range(0, CHUNK, DW_GROUP)])
    o_ref[...] = _conv_tail(cbuf, x_ref[...], bdw_ref[...], lng_ref[...], lnb_ref[...],
                            w2_ref[...], b2_ref[...])


def _conv_meta_kernel(u_ref, x_ref, wdw_ref, bdw_ref, lng_ref, lnb_ref, w2_ref, b2_ref,
                      o_ref, ubuf, cbuf):
    for l in range(SLABS):
        ubuf[l, 0:HALO, :] = jnp.zeros((HALO, LANES), F32)
        ubuf[l, HALO:HALO + N_META, :] = u_ref[:, l * LANES:(l + 1) * LANES]
    _dwconv(ubuf, cbuf, wdw_ref, [(r + CONV_BASE, r) for r in range(0, N_META, DW_GROUP)])
    o_ref[...] = _conv_tail(cbuf, x_ref[...], bdw_ref[...], lng_ref[...], lnb_ref[...],
                            w2_ref[...], b2_ref[...])


def _conv_weight_specs(layer):
    return [_layer_spec((CONV_WIDTH, D_MODEL), layer), _layer_spec((1, D_MODEL), layer),
            _layer_spec((1, D_MODEL), layer), _layer_spec((1, D_MODEL), layer),
            _layer_spec((D_MODEL, D_MODEL), layer), _layer_spec((1, D_MODEL), layer)]


def _conv_layer(x, xm, u, um, state, cw, layer, lay):
    n = x.shape[0]
    out_shape = jax.ShapeDtypeStruct((n, D_MODEL), F32)
    tiles_per_stream = lay["seq"] // ROW_TILE
    n_prompt_tiles = lay["n_prompt"] // ROW_TILE
    slab = lambda rows: pltpu.VMEM((SLABS, rows, LANES), F32)

    x = pl.pallas_call(
        functools.partial(_conv_prompt_kernel, tiles_per_stream=tiles_per_stream),
        grid=(n_prompt_tiles,),
        in_specs=[_rows_spec(ROW_TILE, D_MODEL),
                  pl.BlockSpec((HALO, D_MODEL),
                               lambda i: (jnp.maximum(i * (ROW_TILE // HALO) - 1, 0), 0)),
                  _meta_spec(D_MODEL),
                  _rows_spec(ROW_TILE, D_MODEL)] + _conv_weight_specs(layer),
        out_specs=_rows_spec(ROW_TILE, D_MODEL),
        out_shape=out_shape,
        scratch_shapes=[slab(HALO + ROW_TILE), slab(ROW_TILE)],
        input_output_aliases={3: 0},
        compiler_params=_params(),
        name="conv_prompt",
    )(u, u, um, x, *cw)

    streams = ROW_TILE // CHUNK
    x = pl.pallas_call(
        functools.partial(_conv_sample_kernel, streams=streams),
        grid=(lay["n_sample"] // ROW_TILE,),
        in_specs=[_rows_spec(ROW_TILE, D_MODEL, n_prompt_tiles),
                  pl.BlockSpec((None, streams, CONV_WIDTH - 1, D_MODEL), lambda i: (layer, i, 0, 0)),
                  _rows_spec(ROW_TILE, D_MODEL, n_prompt_tiles)] + _conv_weight_specs(layer),
        out_specs=_rows_spec(ROW_TILE, D_MODEL, n_prompt_tiles),
        out_shape=out_shape,
        scratch_shapes=[slab(streams * (HALO + CHUNK)), slab(ROW_TILE)],
        input_output_aliases={2: 0},
        compiler_params=_params(),
        name="conv_sample",
    )(u, state, x, *cw)

    xm = pl.pallas_call(
        _conv_meta_kernel,
        grid=(1,),
        in_specs=[_meta_spec(D_MODEL), _meta_spec(D_MODEL)] + _conv_weight_specs(layer),
        out_specs=_meta_spec(D_MODEL),
        out_shape=jax.ShapeDtypeStruct((N_META, D_MODEL), F32),
        scratch_shapes=[slab(HALO + N_META), slab(N_META)],
        compiler_params=_params(),
        name="conv_meta",
    )(um, xm, *cw)
    return x, xm


def _split_dot(a, b):
    hi = a.astype(BF16)
    lo = (a - hi.astype(F32)).astype(BF16)
    return _dot(hi, b) + _dot(lo, b)


def _head_mean(sq, b_ref):
    w = b_ref.shape[0]
    parts = [_split_dot(sq[:, j * w:(j + 1) * w], b_ref[...]) for j in range(sq.shape[1] // w)]
    return parts[0] if len(parts) == 1 else jnp.concatenate(parts, axis=1)


def _qkv_kernel(x_ref, xm_ref, g_ref, w_ref, bq_ref, bk_ref, e_ref, gq_ref, gk_ref,
                q_ref, kp_ref, vp_ref, k_ref, v_ref, qm_ref, kpm_ref, vpm_ref, km_ref, vm_ref):
    def body(x):
        h, r = _rms_split(x, g_ref[...])
        qkv = _dot(h, w_ref[...]) * r
        nq = N_HEADS * HEAD_DIM
        q = qkv[:, :nq]
        k = qkv[:, nq:nq + KV_WIDTH]
        v = qkv[:, nq + KV_WIDTH:]
        q = (q * lax.rsqrt(_head_mean(q * q, bq_ref) + EPS)) * gq_ref[...]
        k = (k * lax.rsqrt(_head_mean(k * k, bk_ref) + EPS)) * gk_ref[...]
        return ((q * (HEAD_DIM ** -0.5)).astype(BF16),
                _dot(k.astype(BF16), e_ref[...]).astype(BF16),
                _dot(v.astype(BF16), e_ref[...]).astype(BF16), k, v)

    _tile_and_meta(x_ref, xm_ref, body, (q_ref, kp_ref, vp_ref, k_ref, v_ref),
                   (qm_ref, kpm_ref, vpm_ref, km_ref, vm_ref))


def _qkv(x, xm, g, w, bq, bk, e, gq, gk, layer):
    n = x.shape[0]
    qkv_w = (N_HEADS + 2 * N_KV_HEADS) * HEAD_DIM
    wide = N_KV_HEADS * PAIR_WIDTH
    outs = [(D_MODEL, BF16), (wide, BF16), (wide, BF16), (KV_WIDTH, F32), (KV_WIDTH, F32)]
    return pl.pallas_call(
        _qkv_kernel,
        grid=(n // ROW_TILE,),
        in_specs=[_rows_spec(ROW_TILE, D_MODEL), _meta_spec(D_MODEL), _layer_spec((1, D_MODEL), layer),
                  _layer_spec((D_MODEL, qkv_w), layer), _const_spec((MXU_TILE, MXU_TILE)),
                  _const_spec((KV_WIDTH, KV_WIDTH)), _const_spec((KV_WIDTH, wide)),
                  _const_spec((1, D_MODEL)), _const_spec((1, KV_WIDTH))],
        out_specs=[_rows_spec(ROW_TILE, w_) for w_, _ in outs] + [_meta_spec(w_) for w_, _ in outs],
        out_shape=[jax.ShapeDtypeStruct((n, w_), d_) for w_, d_ in outs]
        + [jax.ShapeDtypeStruct((N_META, w_), d_) for w_, d_ in outs],
        compiler_params=_params(),
        name="qkv",
    )(x, xm, g, w, bq, bk, e, gq, gk)


def _attend(qts, kxs, vxs, sink_row, bias_cols):
    nq = qts[0].shape[0]
    width = N_HEADS * nq
    lane_half = lax.broadcasted_iota(jnp.int32, (nq, PAIR_WIDTH), 1) // HEAD_DIM
    zero = jnp.zeros((nq, PAIR_WIDTH), qts[0].dtype)
    scores = []
    for qt, kx, bias_col in zip(qts, kxs, bias_cols):
        blocks = []
        for h in range(N_HEADS):
            kh, g = divmod(h, GROUP)
            lo = kh * REP_WIDTH + (g // 2) * PAIR_WIDTH
            own = jnp.where(lane_half == g % 2, qt[:, lo:lo + PAIR_WIDTH], jnp.zeros((), qt.dtype))
            blocks.append(jnp.concatenate([own if c == kh else zero for c in range(N_KV_HEADS)], axis=1))
        ql = jnp.concatenate(blocks, axis=0)
        st = lax.dot_general(kx, ql, (((1,), (1,)), ((), ())), preferred_element_type=F32)
        scores.append(st if bias_col is None else st + bias_col)
    st = jnp.concatenate(scores, axis=1)
    sink = jnp.concatenate([sink_row] * len(qts), axis=1)
    m = jnp.maximum(jnp.max(st, axis=0, keepdims=True), sink)
    p = jnp.exp(st - m)
    den = jnp.sum(p, axis=0, keepdims=True) + jnp.exp(sink - m)
    pnt = (p * (1.0 / den)).astype(BF16).T
    outs = []
    for i, vx in enumerate(vxs):
        o2 = _dot(pnt[i * width:(i + 1) * width], vx)
        tiles = []
        for h in range(0, N_HEADS, 2):
            lanes = slice((h // GROUP) * PAIR_WIDTH, (h // GROUP + 1) * PAIR_WIDTH)
            tiles.append(jnp.where(lane_half == 0, o2[h * nq:(h + 1) * nq, lanes],
                                   o2[(h + 1) * nq:(h + 2) * nq, lanes]))
        outs.append(jnp.concatenate(tiles, axis=1))
    return outs


def _attend_small(qg, kx, vx, sink_col):
    nq = qg.shape[0]
    rows = GROUP * nq
    row_head = lax.broadcasted_iota(jnp.int32, (rows, REP_WIDTH), 0) // nq
    lane_head = lax.broadcasted_iota(jnp.int32, (rows, REP_WIDTH), 1) // HEAD_DIM
    qs = jnp.where(row_head == lane_head, jnp.concatenate([qg] * GROUP, axis=0),
                   jnp.zeros((), qg.dtype))
    s = lax.dot_general(qs, kx, (((1,), (1,)), ((), ())), preferred_element_type=F32)
    m = jnp.maximum(jnp.max(s, axis=-1, keepdims=True), sink_col)
    p = jnp.exp(s - m)
    den = jnp.sum(p, axis=-1, keepdims=True) + jnp.exp(sink_col - m)
    pn = (p * (1.0 / den)).astype(BF16)
    ow = _dot(pn, vx)
    out_head = lax.broadcasted_iota(jnp.int32, (nq, REP_WIDTH), 1) // HEAD_DIM
    o = None
    for g in range(GROUP):
        t = jnp.where(out_head == g, ow[g * nq:(g + 1) * nq, :], 0.0)
        o = t if o is None else o + t
    return o


def _attn_prompt_kernel(q_ref, k_ref, khalo_ref, kmeta_ref, v_ref, vhalo_ref, vmeta_ref,
                        x_ref, sink_ref, wo_ref, o_ref, obuf, *, tiles_per_stream):
    chunks = ROW_TILE // CHUNK
    first_chunk = (pl.program_id(0) % tiles_per_stream) * chunks
    key_pos = lax.broadcasted_iota(jnp.int32, (N_KEYS, 1), 0)

    def band(main_ref, halo_ref, meta_ref, c):
        parts = []
        for back in (2, 1, 0):
            j = c - back
            if j < 0:
                parts.append(halo_ref[(2 + j) * CHUNK:(3 + j) * CHUNK, :])
            else:
                parts.append(main_ref[j * CHUNK:(j + 1) * CHUNK, :])
        parts.append(meta_ref[...])
        return jnp.concatenate(parts, axis=0)

    def bias(c):
        if c >= WINDOW // CHUNK:
            return None
        missing = jnp.clip(WINDOW // CHUNK - (first_chunk + c), 0, WINDOW // CHUNK) * CHUNK
        return jnp.where(key_pos < missing, -jnp.inf, 0.0).astype(F32)

    rows = lambda c: slice(c * CHUNK, (c + 1) * CHUNK)
    for c0 in range(0, chunks, ATTN_BATCH):
        cs = range(c0, c0 + ATTN_BATCH)
        outs = _attend([q_ref[rows(c), :] for c in cs],
                       [band(k_ref, khalo_ref, kmeta_ref, c) for c in cs],
                       [band(v_ref, vhalo_ref, vmeta_ref, c) for c in cs],
                       sink_ref[...], [bias(c) for c in cs])
        for c, o in zip(cs, outs):
            obuf[rows(c), :] = o.astype(BF16)
    o_ref[...] = x_ref[...] + _dot(obuf[...], wo_ref[...])


def _attn_sample_kernel(q_ref, k_ref, ck_ref, mk_ref, v_ref, cv_ref, mv_ref,
                        x_ref, sink_ref, wo_ref, o_ref, obuf, *, streams):
    rows = lambda s: slice(s * CHUNK, (s + 1) * CHUNK)
    for s0 in range(0, streams, ATTN_BATCH):
        ss = range(s0, s0 + ATTN_BATCH)
        outs = _attend([q_ref[rows(s), :] for s in ss],
                       [jnp.concatenate([ck_ref[s], k_ref[rows(s), :], mk_ref[s]], axis=0) for s in ss],
                       [jnp.concatenate([cv_ref[s], v_ref[rows(s), :], mv_ref[s]], axis=0) for s in ss],
                       sink_ref[...], [None] * ATTN_BATCH)
        for s, o in zip(ss, outs):
            obuf[rows(s), :] = o.astype(BF16)
    o_ref[...] = x_ref[...] + _dot(obuf[...], wo_ref[...])


def _attn_meta_kernel(q_ref, k_ref, v_ref, x_ref, sink_ref, wo_ref, o_ref, obuf):
    for kh in range(N_KV_HEADS):
        lanes = slice(kh * PAIR_WIDTH, (kh + 1) * PAIR_WIDTH)
        qlanes = slice(kh * REP_WIDTH, (kh + 1) * REP_WIDTH)
        kx = jnp.concatenate([k_ref[:, lanes]] * (GROUP // 2), axis=1)
        vx = jnp.concatenate([v_ref[:, lanes]] * (GROUP // 2), axis=1)
        o = _attend_small(q_ref[:, qlanes], kx, vx,
                          sink_ref[kh * GROUP * N_META:(kh + 1) * GROUP * N_META, :])
        obuf[:, qlanes] = o.astype(BF16)
    o_ref[...] = x_ref[...] + _dot(obuf[...], wo_ref[...])


def _attn_layer(x, xm, q, kp, vp, qm, kpm, vpm, ck, cv, mk, mv, sinks, wo, layer, lay):
    n = x.shape[0]
    out_shape = jax.ShapeDtypeStruct((n, D_MODEL), F32)
    wide = N_KV_HEADS * PAIR_WIDTH
    tiles_per_stream = lay["seq"] // ROW_TILE
    n_prompt_tiles = lay["n_prompt"] // ROW_TILE
    sinks = sinks.astype(F32)
    sink_rows = jnp.repeat(sinks, CHUNK).reshape(1, N_HEADS * CHUNK)
    sink_meta = jnp.repeat(sinks, N_META)[:, None]
    wo_spec = _layer_spec((D_MODEL, D_MODEL), layer)
    sink_spec = _const_spec((1, N_HEADS * CHUNK))

    main = _rows_spec(ROW_TILE, wide)
    halo = pl.BlockSpec((WINDOW, wide), lambda i: (jnp.maximum(i * (ROW_TILE // WINDOW) - 1, 0), 0))
    x = pl.pallas_call(
        functools.partial(_attn_prompt_kernel, tiles_per_stream=tiles_per_stream),
        grid=(n_prompt_tiles,),
        in_specs=[_rows_spec(ROW_TILE, D_MODEL), main, halo, _meta_spec(wide), main, halo, _meta_spec(wide),
                  _rows_spec(ROW_TILE, D_MODEL), sink_spec, wo_spec],
        out_specs=_rows_spec(ROW_TILE, D_MODEL),
        out_shape=out_shape,
        scratch_shapes=[pltpu.VMEM((ROW_TILE, D_MODEL), BF16)],
        input_output_aliases={7: 0},
        compiler_params=_params(),
        name="attn_prompt",
    )(q, kp, kp, kpm, vp, vp, vpm, x, sink_rows, wo)

    streams = ROW_TILE // CHUNK
    smain = _rows_spec(ROW_TILE, wide, n_prompt_tiles)
    cache = pl.BlockSpec((None, streams, WINDOW, wide), lambda i: (layer, i, 0, 0))
    mcache = pl.BlockSpec((None, streams, N_META, wide), lambda i: (layer, i, 0, 0))
    x = pl.pallas_call(
        functools.partial(_attn_sample_kernel, streams=streams),
        grid=(lay["n_sample"] // ROW_TILE,),
        in_specs=[_rows_spec(ROW_TILE, D_MODEL, n_prompt_tiles), smain, cache, mcache, smain, cache, mcache,
                  _rows_spec(ROW_TILE, D_MODEL, n_prompt_tiles), sink_spec, wo_spec],
        out_specs=_rows_spec(ROW_TILE, D_MODEL, n_prompt_tiles),
        out_shape=out_shape,
        scratch_shapes=[pltpu.VMEM((ROW_TILE, D_MODEL), BF16)],
        input_output_aliases={7: 0},
        compiler_params=_params(),
        name="attn_sample",
    )(q, kp, ck, mk, vp, cv, mv, x, sink_rows, wo)

    xm = pl.pallas_call(
        _attn_meta_kernel,
        grid=(1,),
        in_specs=[_meta_spec(D_MODEL), _meta_spec(wide), _meta_spec(wide), _meta_spec(D_MODEL),
                  _const_spec((N_HEADS * N_META, 1)), wo_spec],
        out_specs=_meta_spec(D_MODEL),
        out_shape=jax.ShapeDtypeStruct((N_META, D_MODEL), F32),
        scratch_shapes=[pltpu.VMEM((N_META, D_MODEL), BF16)],
        compiler_params=_params(),
        name="attn_meta",
    )(qm, kpm, vpm, xm, sink_meta, wo)
    return x, xm


def _head_mean_matrix(width):
    idx = np.arange(width) // HEAD_DIM
    return jnp.asarray((idx[:, None] == idx[None, :]).astype(np.float32) / HEAD_DIM, dtype=BF16)


def _pair_matrix():
    src = np.arange(KV_WIDTH)
    dst = np.arange(N_KV_HEADS * PAIR_WIDTH)
    same_head = (src[:, None] // HEAD_DIM) == (dst[None, :] // PAIR_WIDTH)
    same_dim = (src[:, None] % HEAD_DIM) == (dst[None, :] % HEAD_DIM)
    return jnp.asarray((same_head & same_dim).astype(np.float32), dtype=BF16)


def _pair_heads(c):
    c = c.astype(BF16)
    c = jnp.broadcast_to(c[..., :, None, :], c.shape[:-1] + (2, HEAD_DIM))
    return c.reshape(c.shape[:-3] + (N_KV_HEADS * PAIR_WIDTH,))


def kernel(x_prompt, x_sample, cache_swa_k, cache_swa_v, cache_meta_k, cache_meta_v, state_conv, meta_tokens, ffn1_norm, ffn1_w_gate, ffn1_w_up, ffn1_w_down, ffn2_norm, ffn2_w_gate, ffn2_w_up, ffn2_w_down, conv_norm, conv_w_pw1, conv_b_pw1, conv_w_dw, conv_b_dw, conv_ln_g, conv_ln_b, conv_w_pw2, conv_b_pw2, attn_norm, attn_w_qkv, attn_q_gain, attn_k_gain, attn_sinks, attn_w_o, final_norm):
    batch, seq, _ = x_prompt.shape
    dec_batch, dec_seq, _ = x_sample.shape
    assert seq % ROW_TILE == 0 and dec_seq == CHUNK and (dec_batch * dec_seq) % ROW_TILE == 0
    n_prompt = batch * seq
    n_sample = dec_batch * dec_seq
    n_flat = n_prompt + n_sample
    lay = dict(seq=seq, n_prompt=n_prompt, n_sample=n_sample)

    rows3 = lambda a: a.astype(F32)[:, None, :]
    ffn_f32 = ((ffn1_w_gate, ffn1_w_up, ffn1_w_down), (ffn2_w_gate, ffn2_w_up, ffn2_w_down))
    ffn_g = (rows3(ffn1_norm), rows3(ffn2_norm))
    w_cur = [a[0].astype(BF16) for a in ffn_f32[0]]
    conv_w = (conv_w_dw.astype(F32), rows3(conv_b_dw), rows3(conv_ln_g), rows3(conv_ln_b),
              conv_w_pw2.astype(BF16), rows3(conv_b_pw2))
    glu_w = (rows3(conv_norm), conv_w_pw1.astype(BF16), rows3(conv_b_pw1))
    attn_g, w_qkv, w_o = rows3(attn_norm), attn_w_qkv.astype(BF16), attn_w_o.astype(BF16)
    state = state_conv.astype(F32)
    ck, cv = _pair_heads(cache_swa_k), _pair_heads(cache_swa_v)
    mk, mv = _pair_heads(cache_meta_k), _pair_heads(cache_meta_v)
    bq = _head_mean_matrix(MXU_TILE)
    bk = _head_mean_matrix(KV_WIDTH)
    pair = _pair_matrix()
    row = lambda a: a.reshape(1, -1).astype(F32)
    heads = lambda a: a.reshape(a.shape[:-1] + (N_KV_HEADS, HEAD_DIM))
    tail = CONV_WIDTH - 1

    conv_p, conv_s = [], []
    swk_p, swv_p, mk_p, mv_p, swk_s, swv_s = [], [], [], [], [], []
    x = xm = None
    for i in range(DEPTH):
        if i == 0:
            x, xm, w_cur = _ffn_first(x_prompt.reshape(n_prompt, D_MODEL).astype(F32),
                                      x_sample.reshape(n_sample, D_MODEL).astype(F32),
                                      meta_tokens.astype(F32), ffn_g[0], w_cur, i, ffn_f32[1], i, lay)
        else:
            x, xm, w_cur = _ffn(x, xm, ffn_g[0], w_cur, i, ffn_f32[1], i)
        j = i // 2
        if i % 2 == 0:
            u, um = _glu(x, xm, *glu_w, j)
            x, xm = _conv_layer(x, xm, u, um, state, conv_w, j, lay)
            conv_p.append(jnp.stack([u[(b + 1) * seq - tail:(b + 1) * seq] for b in range(batch)]))
            conv_s.append(u[n_prompt:n_flat].reshape(dec_batch, dec_seq, D_MODEL)[:, dec_seq - tail:])
        else:
            q, kp, vp, k, v, qm, kpm, vpm, km, vm = _qkv(
                x, xm, attn_g, w_qkv, bq, bk, pair, row(jnp.tile(attn_q_gain[j], N_HEADS)),
                row(jnp.tile(attn_k_gain[j], N_KV_HEADS)), j)
            x, xm = _attn_layer(x, xm, q, kp, vp, qm, kpm, vpm, ck, cv, mk, mv,
                                attn_sinks[j], w_o, j, lay)
            swk_p.append(jnp.stack([heads(k[(b + 1) * seq - WINDOW:(b + 1) * seq]) for b in range(batch)]))
            swv_p.append(jnp.stack([heads(v[(b + 1) * seq - WINDOW:(b + 1) * seq]) for b in range(batch)]))
            mk_p.append(jnp.broadcast_to(heads(km)[None], (batch, N_META, N_KV_HEADS, HEAD_DIM)))
            mv_p.append(jnp.broadcast_to(heads(vm)[None], (batch, N_META, N_KV_HEADS, HEAD_DIM)))
            ks = heads(k[n_prompt:n_flat].reshape(dec_batch, dec_seq, KV_WIDTH))
            vs = heads(v[n_prompt:n_flat].reshape(dec_batch, dec_seq, KV_WIDTH))
            swk_s.append(jnp.concatenate([cache_swa_k[j].astype(F32)[:, dec_seq:], ks], axis=1))
            swv_s.append(jnp.concatenate([cache_swa_v[j].astype(F32)[:, dec_seq:], vs], axis=1))
        if i < DEPTH - 1:
            x, xm, w_cur = _ffn(x, xm, ffn_g[1], w_cur, i, ffn_f32[0], i + 1)
    y_prompt, y_sample = _ffn_final(x, xm, ffn_g[1], w_cur, row(final_norm), DEPTH - 1, lay)

    return (y_prompt.reshape(batch, seq, D_MODEL), y_sample.reshape(dec_batch, dec_seq, D_MODEL),
            jnp.stack(swk_p), jnp.stack(swv_p), jnp.stack(mk_p), jnp.stack(mv_p), jnp.stack(conv_p),
            jnp.stack(swk_s), jnp.stack(swv_s), jnp.stack(conv_s))
```

```python
import functools

import numpy as np
import jax
import jax.numpy as jnp
from jax import lax
from jax.experimental import pallas as pl
from jax.experimental.pallas import tpu as pltpu

D_MODEL = 1024
D_FF = 2816
DEPTH = 4
N_META = 16
CHUNK = 64
CONV_WIDTH = 31
HEAD_DIM = 64
N_HEADS = 16
N_KV_HEADS = 2
GROUP = 8
WINDOW = 128
EPS = 1e-6

F32 = jnp.float32
BF16 = jnp.bfloat16

LANES = 128
MXU_TILE = 256
SLABS = D_MODEL // LANES
KV_WIDTH = N_KV_HEADS * HEAD_DIM
PAIR_WIDTH = 2 * HEAD_DIM
REP_WIDTH = GROUP * HEAD_DIM
N_KEYS = WINDOW + CHUNK + N_META
HALO = 32
CONV_BASE = HALO - (CONV_WIDTH - 1)
DW_GROUP = 16

ROW_TILE = 512
CAST_GU_STEPS = 32
CAST_DN_STEPS = 16
FF_SPLITS = ((0, 1536), (1536, 2816))

VMEM_LIMIT = 56 * 1024 * 1024


def _params():
    return pltpu.CompilerParams(dimension_semantics=("arbitrary",), vmem_limit_bytes=VMEM_LIMIT)


def _const_spec(shape):
    nd = len(shape)
    return pl.BlockSpec(shape, lambda i: (0,) * nd, pipeline_mode=pl.Buffered(1))


def _layer_spec(shape, layer):
    return pl.BlockSpec((None,) + shape, lambda i: (layer, 0, 0), pipeline_mode=pl.Buffered(1))


def _rows_spec(rows, width, offset_blocks=0):
    return pl.BlockSpec((rows, width), lambda i: (i + offset_blocks, 0))


def _meta_spec(width):
    return pl.BlockSpec((N_META, width), lambda i: (0, 0))


def _rms(x, g):
    ms = jnp.mean(x * x, axis=-1, keepdims=True)
    return (x * lax.rsqrt(ms + EPS)) * g


def _dot(a, b):
    return jnp.dot(a, b, preferred_element_type=F32)


def _rms_split(x, g):
    r = lax.rsqrt(jnp.mean(x * x, axis=-1, keepdims=True) + EPS)
    return (x * g).astype(BF16), r


def _tile_and_meta(x_ref, xm_ref, body, out_refs, meta_refs):
    first = pl.program_id(0) == 0

    @pl.when(first)
    def _():
        outs = body(jnp.concatenate([x_ref[...], xm_ref[...]], axis=0))
        for val, o_ref, m_ref in zip(outs, out_refs, meta_refs):
            o_ref[...] = val[:ROW_TILE]
            if m_ref is not None:
                m_ref[...] = val[ROW_TILE:]

    @pl.when(jnp.logical_not(first))
    def _():
        for val, o_ref in zip(body(x_ref[...]), out_refs):
            o_ref[...] = val


def _swiglu_half(x, g_ref, wg_ref, wu_ref, wd_ref):
    h, r = _rms_split(x, g_ref[...])
    acc = None
    for c0, c1 in FF_SPLITS:
        gate = _dot(h, wg_ref[:, c0:c1]) * r
        up = _dot(h, wu_ref[:, c0:c1]) * r
        a = (gate * jax.nn.sigmoid(gate) * up).astype(BF16)
        part = _dot(a, wd_ref[c0:c1, :])
        acc = part if acc is None else acc + part
    return x + 0.5 * acc


def _cast_next(next_refs, out_refs):
    for n_ref, o_ref in zip(next_refs, out_refs):
        o_ref[...] = n_ref[...].astype(BF16)


def _ffn_kernel(x_ref, xm_ref, g_ref, wg_ref, wu_ref, wd_ref, *refs, n_cast):
    o_ref, om_ref = refs[n_cast:n_cast + 2]
    _cast_next(refs[:n_cast], refs[n_cast + 2:])
    body = lambda x: (_swiglu_half(x, g_ref, wg_ref, wu_ref, wd_ref),)
    _tile_and_meta(x_ref, xm_ref, body, (o_ref,), (om_ref,))


def _ffn_first_kernel(xp_ref, xs_ref, xm_ref, g_ref, wg_ref, wu_ref, wd_ref, *refs, n_cast, prompt_tiles):
    i = pl.program_id(0)
    o_ref, om_ref = refs[n_cast:n_cast + 2]
    xbuf = refs[-1]
    _cast_next(refs[:n_cast], refs[n_cast + 2:-1])

    @pl.when(i < prompt_tiles)
    def _():
        xbuf[...] = xp_ref[...]

    @pl.when(i >= prompt_tiles)
    def _():
        xbuf[...] = xs_ref[...]

    body = lambda x: (_swiglu_half(x, g_ref, wg_ref, wu_ref, wd_ref),)
    _tile_and_meta(xbuf, xm_ref, body, (o_ref,), (om_ref,))


def _ffn_final_kernel(x_ref, xm_ref, g_ref, wg_ref, wu_ref, wd_ref, fg_ref, yp_ref, ys_ref, ybuf,
                      *, prompt_tiles):
    i = pl.program_id(0)
    body = lambda x: (_rms(_swiglu_half(x, g_ref, wg_ref, wu_ref, wd_ref), fg_ref[...]),)
    _tile_and_meta(x_ref, xm_ref, body, (ybuf,), (None,))

    @pl.when(i < prompt_tiles)
    def _():
        yp_ref[...] = ybuf[...]

    @pl.when(i >= prompt_tiles)
    def _():
        ys_ref[...] = ybuf[...]


def _ffn_weight_specs(layer):
    return [_layer_spec((1, D_MODEL), layer), _const_spec((D_MODEL, D_FF)),
            _const_spec((D_MODEL, D_FF)), _const_spec((D_FF, D_MODEL))]


def _ffn_cast_items(layer):
    return [(D_MODEL, D_FF, layer, CAST_GU_STEPS), (D_MODEL, D_FF, layer, CAST_GU_STEPS),
            (D_FF, D_MODEL, layer, CAST_DN_STEPS)]


def _next_weight_specs(items, steps):
    in_specs, out_specs, shapes = [], [], []
    for rows, cols, layer, walk in items:
        assert steps >= walk and rows % walk == 0
        idx = functools.partial(lambda i, walk: jnp.minimum(i, walk - 1), walk=walk)
        in_specs.append(pl.BlockSpec((None, rows // walk, cols),
                                     functools.partial(lambda i, layer, idx: (layer, idx(i), 0), layer=layer, idx=idx)))
        out_specs.append(pl.BlockSpec((rows // walk, cols),
                                      functools.partial(lambda i, idx: (idx(i), 0), idx=idx)))
        shapes.append(jax.ShapeDtypeStruct((rows, cols), BF16))
    return in_specs, out_specs, shapes


def _flat_out(n, width, dtype):
    return ([_rows_spec(ROW_TILE, width), _meta_spec(width)],
            [jax.ShapeDtypeStruct((n, width), dtype), jax.ShapeDtypeStruct((N_META, width), dtype)])


def _ffn(x, xm, g, w, layer, nxt, items):
    n = x.shape[0]
    out_specs, out_shape = _flat_out(n, D_MODEL, F32)
    n_in, n_out, n_shape = _next_weight_specs(items, n // ROW_TILE)
    x, xm, *w_next = pl.pallas_call(
        functools.partial(_ffn_kernel, n_cast=len(items)),
        grid=(n // ROW_TILE,),
        in_specs=[_rows_spec(ROW_TILE, D_MODEL), _meta_spec(D_MODEL)] + _ffn_weight_specs(layer) + n_in,
        out_specs=out_specs + n_out,
        out_shape=out_shape + n_shape,
        compiler_params=_params(),
        name="ffn",
    )(x, xm, g, *w, *nxt)
    return x, xm, w_next


def _ffn_first(xp, xs, xm, g, w, layer, nxt, items, lay):
    pt, st = lay["n_prompt"] // ROW_TILE, lay["n_sample"] // ROW_TILE
    n = lay["n_prompt"] + lay["n_sample"]
    out_specs, out_shape = _flat_out(n, D_MODEL, F32)
    n_in, n_out, n_shape = _next_weight_specs(items, n // ROW_TILE)
    x, xm, *w_next = pl.pallas_call(
        functools.partial(_ffn_first_kernel, n_cast=len(items), prompt_tiles=pt),
        grid=(n // ROW_TILE,),
        in_specs=[pl.BlockSpec((ROW_TILE, D_MODEL), lambda i: (jnp.minimum(i, pt - 1), 0)),
                  pl.BlockSpec((ROW_TILE, D_MODEL), lambda i: (jnp.clip(i - pt, 0, st - 1), 0)),
                  _meta_spec(D_MODEL)] + _ffn_weight_specs(layer) + n_in,
        out_specs=out_specs + n_out,
        out_shape=out_shape + n_shape,
        scratch_shapes=[pltpu.VMEM((ROW_TILE, D_MODEL), F32)],
        compiler_params=_params(),
        name="ffn_first",
    )(xp, xs, xm, g, *w, *nxt)
    return x, xm, w_next


def _ffn_final(x, xm, g, w, fg, layer, lay):
    pt, st = lay["n_prompt"] // ROW_TILE, lay["n_sample"] // ROW_TILE
    n = x.shape[0]
    return pl.pallas_call(
        functools.partial(_ffn_final_kernel, prompt_tiles=pt),
        grid=(n // ROW_TILE,),
        in_specs=[_rows_spec(ROW_TILE, D_MODEL), _meta_spec(D_MODEL)] + _ffn_weight_specs(layer)
        + [_const_spec((1, D_MODEL))],
        out_specs=[pl.BlockSpec((ROW_TILE, D_MODEL), lambda i: (jnp.minimum(i, pt - 1), 0)),
                   pl.BlockSpec((ROW_TILE, D_MODEL), lambda i: (jnp.clip(i - pt, 0, st - 1), 0))],
        out_shape=[jax.ShapeDtypeStruct((lay["n_prompt"], D_MODEL), F32),
                   jax.ShapeDtypeStruct((lay["n_sample"], D_MODEL), F32)],
        scratch_shapes=[pltpu.VMEM((ROW_TILE, D_MODEL), F32)],
        compiler_params=_params(),
        name="ffn_final",
    )(x, xm, g, *w, fg)


def _glu_kernel(x_ref, xm_ref, g_ref, w_ref, b_ref, u_ref, um_ref):
    def body(x):
        h, r = _rms_split(x, g_ref[...])
        a = _dot(h, w_ref[...]) * r + b_ref[...]
        return (a[:, :D_MODEL] * jax.nn.sigmoid(a[:, D_MODEL:]),)

    _tile_and_meta(x_ref, xm_ref, body, (u_ref,), (um_ref,))


def _glu(x, xm, g, w, b, layer):
    n = x.shape[0]
    out_specs, out_shape = _flat_out(n, D_MODEL, F32)
    return pl.pallas_call(
        _glu_kernel,
        grid=(n // ROW_TILE,),
        in_specs=[_rows_spec(ROW_TILE, D_MODEL), _meta_spec(D_MODEL), _layer_spec((1, D_MODEL), layer),
                  _const_spec((D_MODEL, 2 * D_MODEL)), _layer_spec((1, 2 * D_MODEL), layer)],
        out_specs=out_specs,
        out_shape=out_shape,
        compiler_params=_params(),
        name="glu",
    )(x, xm, g, w, b)


def _dwconv(ubuf, cbuf, wdw_ref, groups):
    for l in range(SLABS):
        lanes = slice(l * LANES, (l + 1) * LANES)
        for src, dst in groups:
            loads = [ubuf[l, pl.ds(src + j, 8, stride=2), :] for j in range(CONV_WIDTH + 1)]
            acc_e = acc_o = None
            for k in range(CONV_WIDTH):
                w = wdw_ref[k:k + 1, lanes]
                te, to = loads[k] * w, loads[k + 1] * w
                acc_e = te if acc_e is None else acc_e + te
                acc_o = to if acc_o is None else acc_o + to
            cbuf[l, pl.ds(dst, 8, stride=2), :] = acc_e
            cbuf[l, pl.ds(dst + 1, 8, stride=2), :] = acc_o


def _conv_tail(cbuf, x, bdw, lng, lnb, w2, b2):
    c = jnp.concatenate([cbuf[l] for l in range(SLABS)], axis=1) + bdw
    mu = jnp.mean(c, axis=-1, keepdims=True)
    cc = c - mu
    var = jnp.mean(cc * cc, axis=-1, keepdims=True)
    y = cc * lax.rsqrt(var + EPS) * lng + lnb
    y = (y * jax.nn.sigmoid(y)).astype(BF16)
    return x + (_dot(y, w2) + b2)


def _conv_prompt_kernel(u_ref, halo_ref, umeta_ref, x_ref, wdw_ref, bdw_ref, lng_ref, lnb_ref,
                        w2_ref, b2_ref, o_ref, ubuf, cbuf, *, tiles_per_stream):
    first = pl.program_id(0) % tiles_per_stream == 0

    @pl.when(first)
    def _():
        for l in range(SLABS):
            ubuf[l, 0:HALO - N_META, :] = jnp.zeros((HALO - N_META, LANES), F32)
            ubuf[l, HALO - N_META:HALO, :] = umeta_ref[:, l * LANES:(l + 1) * LANES]

    @pl.when(jnp.logical_not(first))
    def _():
        for l in range(SLABS):
            ubuf[l, 0:HALO, :] = halo_ref[:, l * LANES:(l + 1) * LANES]

    for l in range(SLABS):
        ubuf[l, HALO:HALO + ROW_TILE, :] = u_ref[:, l * LANES:(l + 1) * LANES]
    _dwconv(ubuf, cbuf, wdw_ref, [(r + CONV_BASE, r) for r in range(0, ROW_TILE, DW_GROUP)])
    o_ref[...] = _conv_tail(cbuf, x_ref[...], bdw_ref[...], lng_ref[...], lnb_ref[...],
                            w2_ref[...], b2_ref[...])


def _conv_sample_kernel(u_ref, st_ref, x_ref, wdw_ref, bdw_ref, lng_ref, lnb_ref,
                        w2_ref, b2_ref, o_ref, ubuf, cbuf, *, streams):
    span = HALO + CHUNK
    for s in range(streams):
        for l in range(SLABS):
            lanes = slice(l * LANES, (l + 1) * LANES)
            ubuf[l, s * span + CONV_BASE:s * span + HALO, :] = st_ref[s, :, lanes]
            ubuf[l, s * span + HALO:(s + 1) * span, :] = u_ref[s * CHUNK:(s + 1) * CHUNK, lanes]
    _dwconv(ubuf, cbuf, wdw_ref, [(s * span + r + CONV_BASE, s * CHUNK + r)
                                  for s in range(streams) for r in range(0, CHUNK, DW_GROUP)])
    o_ref[...] = _conv_tail(cbuf, x_ref[...], bdw_ref[...], lng_ref[...], lnb_ref[...],
                            w2_ref[...], b2_ref[...])


def _conv_meta_kernel(u_ref, x_ref, wdw_ref, bdw_ref, lng_ref, lnb_ref, w2_ref, b2_ref,
                      o_ref, ubuf, cbuf):
    for l in range(SLABS):
        ubuf[l, 0:HALO, :] = jnp.zeros((HALO, LANES), F32)
        ubuf[l, HALO:HALO + N_META, :] = u_ref[:, l * LANES:(l + 1) * LANES]
    _dwconv(ubuf, cbuf, wdw_ref, [(r + CONV_BASE, r) for r in range(0, N_META, DW_GROUP)])
    o_ref[...] = _conv_tail(cbuf, x_ref[...], bdw_ref[...], lng_ref[...], lnb_ref[...],
                            w2_ref[...], b2_ref[...])


def _conv_weight_specs(layer):
    return [_layer_spec((CONV_WIDTH, D_MODEL), layer), _layer_spec((1, D_MODEL), layer),
            _layer_spec((1, D_MODEL), layer), _layer_spec((1, D_MODEL), layer),
            _const_spec((D_MODEL, D_MODEL)), _layer_spec((1, D_MODEL), layer)]


def _conv_layer(x, xm, u, um, state, cw, layer, lay):
    n = x.shape[0]
    out_shape = jax.ShapeDtypeStruct((n, D_MODEL), F32)
    tiles_per_stream = lay["seq"] // ROW_TILE
    n_prompt_tiles = lay["n_prompt"] // ROW_TILE
    slab = lambda rows: pltpu.VMEM((SLABS, rows, LANES), F32)

    x = pl.pallas_call(
        functools.partial(_conv_prompt_kernel, tiles_per_stream=tiles_per_stream),
        grid=(n_prompt_tiles,),
        in_specs=[_rows_spec(ROW_TILE, D_MODEL),
                  pl.BlockSpec((HALO, D_MODEL),
                               lambda i: (jnp.maximum(i * (ROW_TILE // HALO) - 1, 0), 0)),
                  _meta_spec(D_MODEL),
                  _rows_spec(ROW_TILE, D_MODEL)] + _conv_weight_specs(layer),
        out_specs=_rows_spec(ROW_TILE, D_MODEL),
        out_shape=out_shape,
        scratch_shapes=[slab(HALO + ROW_TILE), slab(ROW_TILE)],
        input_output_aliases={3: 0},
        compiler_params=_params(),
        name="conv_prompt",
    )(u, u, um, x, *cw)

    streams = ROW_TILE // CHUNK
    x = pl.pallas_call(
        functools.partial(_conv_sample_kernel, streams=streams),
        grid=(lay["n_sample"] // ROW_TILE,),
        in_specs=[_rows_spec(ROW_TILE, D_MODEL, n_prompt_tiles),
                  pl.BlockSpec((None, streams, CONV_WIDTH - 1, D_MODEL), lambda i: (layer, i, 0, 0)),
                  _rows_spec(ROW_TILE, D_MODEL, n_prompt_tiles)] + _conv_weight_specs(layer),
        out_specs=_rows_spec(ROW_TILE, D_MODEL, n_prompt_tiles),
        out_shape=out_shape,
        scratch_shapes=[slab(streams * (HALO + CHUNK)), slab(ROW_TILE)],
        input_output_aliases={2: 0},
        compiler_params=_params(),
        name="conv_sample",
    )(u, state, x, *cw)

    xm = pl.pallas_call(
        _conv_meta_kernel,
        grid=(1,),
        in_specs=[_meta_spec(D_MODEL), _meta_spec(D_MODEL)] + _conv_weight_specs(layer),
        out_specs=_meta_spec(D_MODEL),
        out_shape=jax.ShapeDtypeStruct((N_META, D_MODEL), F32),
        scratch_shapes=[slab(HALO + N_META), slab(N_META)],
        compiler_params=_params(),
        name="conv_meta",
    )(um, xm, *cw)
    return x, xm


def _split_dot(a, b):
    hi = a.astype(BF16)
    lo = (a - hi.astype(F32)).astype(BF16)
    return _dot(hi, b) + _dot(lo, b)


def _head_mean(sq, b_ref):
    w = b_ref.shape[0]
    parts = [_split_dot(sq[:, j * w:(j + 1) * w], b_ref[...]) for j in range(sq.shape[1] // w)]
    return parts[0] if len(parts) == 1 else jnp.concatenate(parts, axis=1)


def _qkv_kernel(x_ref, xm_ref, g_ref, w_ref, bq_ref, bk_ref, e_ref, gq_ref, gk_ref,
                q_ref, kp_ref, vp_ref, k_ref, v_ref, qm_ref, kpm_ref, vpm_ref, km_ref, vm_ref):
    def body(x):
        h, r = _rms_split(x, g_ref[...])
        qkv = _dot(h, w_ref[...]) * r
        nq = N_HEADS * HEAD_DIM
        q = qkv[:, :nq]
        k = qkv[:, nq:nq + KV_WIDTH]
        v = qkv[:, nq + KV_WIDTH:]
        q = (q * lax.rsqrt(_head_mean(q * q, bq_ref) + EPS)) * gq_ref[...]
        k = (k * lax.rsqrt(_head_mean(k * k, bk_ref) + EPS)) * gk_ref[...]
        return ((q * (HEAD_DIM ** -0.5)).astype(BF16),
                _dot(k.astype(BF16), e_ref[...]).astype(BF16),
                _dot(v.astype(BF16), e_ref[...]).astype(BF16), k, v)

    _tile_and_meta(x_ref, xm_ref, body, (q_ref, kp_ref, vp_ref, k_ref, v_ref),
                   (qm_ref, kpm_ref, vpm_ref, km_ref, vm_ref))


def _qkv(x, xm, g, w, bq, bk, e, gq, gk, layer):
    n = x.shape[0]
    qkv_w = (N_HEADS + 2 * N_KV_HEADS) * HEAD_DIM
    wide = N_KV_HEADS * PAIR_WIDTH
    outs = [(D_MODEL, BF16), (wide, BF16), (wide, BF16), (KV_WIDTH, F32), (KV_WIDTH, F32)]
    return pl.pallas_call(
        _qkv_kernel,
        grid=(n // ROW_TILE,),
        in_specs=[_rows_spec(ROW_TILE, D_MODEL), _meta_spec(D_MODEL), _layer_spec((1, D_MODEL), layer),
                  _const_spec((D_MODEL, qkv_w)), _const_spec((MXU_TILE, MXU_TILE)),
                  _const_spec((KV_WIDTH, KV_WIDTH)), _const_spec((KV_WIDTH, wide)),
                  _const_spec((1, D_MODEL)), _const_spec((1, KV_WIDTH))],
        out_specs=[_rows_spec(ROW_TILE, w_) for w_, _ in outs] + [_meta_spec(w_) for w_, _ in outs],
        out_shape=[jax.ShapeDtypeStruct((n, w_), d_) for w_, d_ in outs]
        + [jax.ShapeDtypeStruct((N_META, w_), d_) for w_, d_ in outs],
        compiler_params=_params(),
        name="qkv",
    )(x, xm, g, w, bq, bk, e, gq, gk)


def _attend(qt, kx, vx, sink_row, bias_col):
    nq = qt.shape[0]
    lane_half = lax.broadcasted_iota(jnp.int32, (nq, PAIR_WIDTH), 1) // HEAD_DIM
    zero = jnp.zeros((nq, PAIR_WIDTH), qt.dtype)
    blocks = []
    for h in range(N_HEADS):
        kh, g = divmod(h, GROUP)
        lo = kh * REP_WIDTH + (g // 2) * PAIR_WIDTH
        own = jnp.where(lane_half == g % 2, qt[:, lo:lo + PAIR_WIDTH], jnp.zeros((), qt.dtype))
        blocks.append(jnp.concatenate([own if c == kh else zero for c in range(N_KV_HEADS)], axis=1))
    ql = jnp.concatenate(blocks, axis=0)
    st = lax.dot_general(kx, ql, (((1,), (1,)), ((), ())), preferred_element_type=F32)
    if bias_col is not None:
        st = st + bias_col
    m = jnp.maximum(jnp.max(st, axis=0, keepdims=True), sink_row)
    p = jnp.exp(st - m)
    den = jnp.sum(p, axis=0, keepdims=True) + jnp.exp(sink_row - m)
    pn = (p * (1.0 / den)).astype(BF16)
    o2 = _dot(pn.T, vx)
    tiles = []
    for h in range(0, N_HEADS, 2):
        lanes = slice((h // GROUP) * PAIR_WIDTH, (h // GROUP + 1) * PAIR_WIDTH)
        tiles.append(jnp.where(lane_half == 0, o2[h * nq:(h + 1) * nq, lanes],
                               o2[(h + 1) * nq:(h + 2) * nq, lanes]))
    return jnp.concatenate(tiles, axis=1)


def _attend_small(qg, kx, vx, sink_col):
    nq = qg.shape[0]
    rows = GROUP * nq
    row_head = lax.broadcasted_iota(jnp.int32, (rows, REP_WIDTH), 0) // nq
    lane_head = lax.broadcasted_iota(jnp.int32, (rows, REP_WIDTH), 1) // HEAD_DIM
    qs = jnp.where(row_head == lane_head, jnp.concatenate([qg] * GROUP, axis=0),
                   jnp.zeros((), qg.dtype))
    s = lax.dot_general(qs, kx, (((1,), (1,)), ((), ())), preferred_element_type=F32)
    m = jnp.maximum(jnp.max(s, axis=-1, keepdims=True), sink_col)
    p = jnp.exp(s - m)
    den = jnp.sum(p, axis=-1, keepdims=True) + jnp.exp(sink_col - m)
    pn = (p * (1.0 / den)).astype(BF16)
    ow = _dot(pn, vx)
    out_head = lax.broadcasted_iota(jnp.int32, (nq, REP_WIDTH), 1) // HEAD_DIM
    o = None
    for g in range(GROUP):
        t = jnp.where(out_head == g, ow[g * nq:(g + 1) * nq, :], 0.0)
        o = t if o is None else o + t
    return o


def _attn_prompt_kernel(q_ref, k_ref, khalo_ref, kmeta_ref, v_ref, vhalo_ref, vmeta_ref,
                        x_ref, sink_ref, wo_ref, o_ref, obuf, *, tiles_per_stream):
    chunks = ROW_TILE // CHUNK
    first_chunk = (pl.program_id(0) % tiles_per_stream) * chunks
    key_pos = lax.broadcasted_iota(jnp.int32, (N_KEYS, 1), 0)

    def band(main_ref, halo_ref, meta_ref, c):
        parts = []
        for back in (2, 1, 0):
            j = c - back
            if j < 0:
                parts.append(halo_ref[(2 + j) * CHUNK:(3 + j) * CHUNK, :])
            else:
                parts.append(main_ref[j * CHUNK:(j + 1) * CHUNK, :])
        parts.append(meta_ref[...])
        return jnp.concatenate(parts, axis=0)

    for c in range(chunks):
        bias = None
        if c < WINDOW // CHUNK:
            missing = jnp.clip(WINDOW // CHUNK - (first_chunk + c), 0, WINDOW // CHUNK) * CHUNK
            bias = jnp.where(key_pos < missing, -jnp.inf, 0.0).astype(F32)
        rows = slice(c * CHUNK, (c + 1) * CHUNK)
        o = _attend(q_ref[rows, :], band(k_ref, khalo_ref, kmeta_ref, c),
                    band(v_ref, vhalo_ref, vmeta_ref, c), sink_ref[...], bias)
        obuf[rows, :] = o.astype(BF16)
    o_ref[...] = x_ref[...] + _dot(obuf[...], wo_ref[...])


def _attn_sample_kernel(q_ref, k_ref, ck_ref, mk_ref, v_ref, cv_ref, mv_ref,
                        x_ref, sink_ref, wo_ref, o_ref, obuf, *, streams):
    for s in range(streams):
        rows = slice(s * CHUNK, (s + 1) * CHUNK)
        kx = jnp.concatenate([ck_ref[s], k_ref[rows, :], mk_ref[s]], axis=0)
        vx = jnp.concatenate([cv_ref[s], v_ref[rows, :], mv_ref[s]], axis=0)
        obuf[rows, :] = _attend(q_ref[rows, :], kx, vx, sink_ref[...], None).astype(BF16)
    o_ref[...] = x_ref[...] + _dot(obuf[...], wo_ref[...])


def _attn_meta_kernel(q_ref, k_ref, v_ref, x_ref, sink_ref, wo_ref, o_ref, obuf):
    for kh in range(N_KV_HEADS):
        lanes = slice(kh * PAIR_WIDTH, (kh + 1) * PAIR_WIDTH)
        qlanes = slice(kh * REP_WIDTH, (kh + 1) * REP_WIDTH)
        kx = jnp.concatenate([k_ref[:, lanes]] * (GROUP // 2), axis=1)
        vx = jnp.concatenate([v_ref[:, lanes]] * (GROUP // 2), axis=1)
        o = _attend_small(q_ref[:, qlanes], kx, vx,
                          sink_ref[kh * GROUP * N_META:(kh + 1) * GROUP * N_META, :])
        obuf[:, qlanes] = o.astype(BF16)
    o_ref[...] = x_ref[...] + _dot(obuf[...], wo_ref[...])


def _attn_layer(x, xm, q, kp, vp, qm, kpm, vpm, ck, cv, mk, mv, sinks, wo, layer, lay):
    n = x.shape[0]
    out_shape = jax.ShapeDtypeStruct((n, D_MODEL), F32)
    wide = N_KV_HEADS * PAIR_WIDTH
    tiles_per_stream = lay["seq"] // ROW_TILE
    n_prompt_tiles = lay["n_prompt"] // ROW_TILE
    sinks = sinks.astype(F32)
    sink_rows = jnp.repeat(sinks, CHUNK).reshape(1, N_HEADS * CHUNK)
    sink_meta = jnp.repeat(sinks, N_META)[:, None]
    wo_spec = _const_spec((D_MODEL, D_MODEL))
    sink_spec = _const_spec((1, N_HEADS * CHUNK))

    main = _rows_spec(ROW_TILE, wide)
    halo = pl.BlockSpec((WINDOW, wide), lambda i: (jnp.maximum(i * (ROW_TILE // WINDOW) - 1, 0), 0))
    x = pl.pallas_call(
        functools.partial(_attn_prompt_kernel, tiles_per_stream=tiles_per_stream),
        grid=(n_prompt_tiles,),
        in_specs=[_rows_spec(ROW_TILE, D_MODEL), main, halo, _meta_spec(wide), main, halo, _meta_spec(wide),
                  _rows_spec(ROW_TILE, D_MODEL), sink_spec, wo_spec],
        out_specs=_rows_spec(ROW_TILE, D_MODEL),
        out_shape=out_shape,
        scratch_shapes=[pltpu.VMEM((ROW_TILE, D_MODEL), BF16)],
        input_output_aliases={7: 0},
        compiler_params=_params(),
        name="attn_prompt",
    )(q, kp, kp, kpm, vp, vp, vpm, x, sink_rows, wo)

    streams = ROW_TILE // CHUNK
    smain = _rows_spec(ROW_TILE, wide, n_prompt_tiles)
    cache = pl.BlockSpec((None, streams, WINDOW, wide), lambda i: (layer, i, 0, 0))
    mcache = pl.BlockSpec((None, streams, N_META, wide), lambda i: (layer, i, 0, 0))
    x = pl.pallas_call(
        functools.partial(_attn_sample_kernel, streams=streams),
        grid=(lay["n_sample"] // ROW_TILE,),
        in_specs=[_rows_spec(ROW_TILE, D_MODEL, n_prompt_tiles), smain, cache, mcache, smain, cache, mcache,
                  _rows_spec(ROW_TILE, D_MODEL, n_prompt_tiles), sink_spec, wo_spec],
        out_specs=_rows_spec(ROW_TILE, D_MODEL, n_prompt_tiles),
        out_shape=out_shape,
        scratch_shapes=[pltpu.VMEM((ROW_TILE, D_MODEL), BF16)],
        input_output_aliases={7: 0},
        compiler_params=_params(),
        name="attn_sample",
    )(q, kp, ck, mk, vp, cv, mv, x, sink_rows, wo)

    xm = pl.pallas_call(
        _attn_meta_kernel,
        grid=(1,),
        in_specs=[_meta_spec(D_MODEL), _meta_spec(wide), _meta_spec(wide), _meta_spec(D_MODEL),
                  _const_spec((N_HEADS * N_META, 1)), wo_spec],
        out_specs=_meta_spec(D_MODEL),
        out_shape=jax.ShapeDtypeStruct((N_META, D_MODEL), F32),
        scratch_shapes=[pltpu.VMEM((N_META, D_MODEL), BF16)],
        compiler_params=_params(),
        name="attn_meta",
    )(qm, kpm, vpm, xm, sink_meta, wo)
    return x, xm


def _head_mean_matrix(width):
    idx = np.arange(width) // HEAD_DIM
    return jnp.asarray((idx[:, None] == idx[None, :]).astype(np.float32) / HEAD_DIM, dtype=BF16)


def _pair_matrix():
    src = np.arange(KV_WIDTH)
    dst = np.arange(N_KV_HEADS * PAIR_WIDTH)
    same_head = (src[:, None] // HEAD_DIM) == (dst[None, :] // PAIR_WIDTH)
    same_dim = (src[:, None] % HEAD_DIM) == (dst[None, :] % HEAD_DIM)
    return jnp.asarray((same_head & same_dim).astype(np.float32), dtype=BF16)


def _pair_heads(c):
    c = c.astype(BF16)
    c = jnp.broadcast_to(c[..., :, None, :], c.shape[:-1] + (2, HEAD_DIM))
    return c.reshape(c.shape[:-3] + (N_KV_HEADS * PAIR_WIDTH,))


def kernel(x_prompt, x_sample, cache_swa_k, cache_swa_v, cache_meta_k, cache_meta_v, state_conv, meta_tokens, ffn1_norm, ffn1_w_gate, ffn1_w_up, ffn1_w_down, ffn2_norm, ffn2_w_gate, ffn2_w_up, ffn2_w_down, conv_norm, conv_w_pw1, conv_b_pw1, conv_w_dw, conv_b_dw, conv_ln_g, conv_ln_b, conv_w_pw2, conv_b_pw2, attn_norm, attn_w_qkv, attn_q_gain, attn_k_gain, attn_sinks, attn_w_o, final_norm):
    batch, seq, _ = x_prompt.shape
    dec_batch, dec_seq, _ = x_sample.shape
    assert seq % ROW_TILE == 0 and dec_seq == CHUNK and (dec_batch * dec_seq) % ROW_TILE == 0
    n_prompt = batch * seq
    n_sample = dec_batch * dec_seq
    n_flat = n_prompt + n_sample
    lay = dict(seq=seq, n_prompt=n_prompt, n_sample=n_sample)

    rows3 = lambda a: a.astype(F32)[:, None, :]
    ffn_f32 = ((ffn1_w_gate, ffn1_w_up, ffn1_w_down), (ffn2_w_gate, ffn2_w_up, ffn2_w_down))
    ffn_g = (rows3(ffn1_norm), rows3(ffn2_norm))
    w_cur = [a[0].astype(BF16) for a in ffn_f32[0]]
    conv_w = lambda w2: (conv_w_dw.astype(F32), rows3(conv_b_dw), rows3(conv_ln_g), rows3(conv_ln_b),
                         w2, rows3(conv_b_pw2))
    glu_g, glu_b, attn_g = rows3(conv_norm), rows3(conv_b_pw1), rows3(attn_norm)
    mixer_f32 = ((conv_w_pw1, conv_w_pw2), (attn_w_qkv, attn_w_o))
    mixer_cols = (2 * D_MODEL, (N_HEADS + 2 * N_KV_HEADS) * HEAD_DIM)
    mixer_items = lambda i: [(D_MODEL, mixer_cols[i % 2], i // 2, CAST_GU_STEPS),
                             (D_MODEL, D_MODEL, i // 2, CAST_GU_STEPS)]
    state = state_conv.astype(F32)
    ck, cv = _pair_heads(cache_swa_k), _pair_heads(cache_swa_v)
    mk, mv = _pair_heads(cache_meta_k), _pair_heads(cache_meta_v)
    bq = _head_mean_matrix(MXU_TILE)
    bk = _head_mean_matrix(KV_WIDTH)
    pair = _pair_matrix()
    row = lambda a: a.reshape(1, -1).astype(F32)
    heads = lambda a: a.reshape(a.shape[:-1] + (N_KV_HEADS, HEAD_DIM))
    tail = CONV_WIDTH - 1

    conv_p, conv_s = [], []
    swk_p, swv_p, mk_p, mv_p, swk_s, swv_s = [], [], [], [], [], []
    x = xm = None
    for i in range(DEPTH):
        nxt, items = ffn_f32[1] + mixer_f32[i % 2], _ffn_cast_items(i) + mixer_items(i)
        if i == 0:
            x, xm, cast = _ffn_first(x_prompt.reshape(n_prompt, D_MODEL).astype(F32),
                                     x_sample.reshape(n_sample, D_MODEL).astype(F32),
                                     meta_tokens.astype(F32), ffn_g[0], w_cur, i, nxt, items, lay)
        else:
            x, xm, cast = _ffn(x, xm, ffn_g[0], w_cur, i, nxt, items)
        w_cur, (w_in, w_out) = cast[:3], cast[3:]
        j = i // 2
        if i % 2 == 0:
            u, um = _glu(x, xm, glu_g, w_in, glu_b, j)
            x, xm = _conv_layer(x, xm, u, um, state, conv_w(w_out), j, lay)
            conv_p.append(jnp.stack([u[(b + 1) * seq - tail:(b + 1) * seq] for b in range(batch)]))
            conv_s.append(u[n_prompt:n_flat].reshape(dec_batch, dec_seq, D_MODEL)[:, dec_seq - tail:])
        else:
            q, kp, vp, k, v, qm, kpm, vpm, km, vm = _qkv(
                x, xm, attn_g, w_in, bq, bk, pair, row(jnp.tile(attn_q_gain[j], N_HEADS)),
                row(jnp.tile(attn_k_gain[j], N_KV_HEADS)), j)
            x, xm = _attn_layer(x, xm, q, kp, vp, qm, kpm, vpm, ck, cv, mk, mv,
                                attn_sinks[j], w_out, j, lay)
            swk_p.append(jnp.stack([heads(k[(b + 1) * seq - WINDOW:(b + 1) * seq]) for b in range(batch)]))
            swv_p.append(jnp.stack([heads(v[(b + 1) * seq - WINDOW:(b + 1) * seq]) for b in range(batch)]))
            mk_p.append(jnp.broadcast_to(heads(km)[None], (batch, N_META, N_KV_HEADS, HEAD_DIM)))
            mv_p.append(jnp.broadcast_to(heads(vm)[None], (batch, N_META, N_KV_HEADS, HEAD_DIM)))
            ks = heads(k[n_prompt:n_flat].reshape(dec_batch, dec_seq, KV_WIDTH))
            vs = heads(v[n_prompt:n_flat].reshape(dec_batch, dec_seq, KV_WIDTH))
            swk_s.append(jnp.concatenate([cache_swa_k[j].astype(F32)[:, dec_seq:], ks], axis=1))
            swv_s.append(jnp.concatenate([cache_swa_v[j].astype(F32)[:, dec_seq:], vs], axis=1))
        if i < DEPTH - 1:
            x, xm, w_cur = _ffn(x, xm, ffn_g[1], w_cur, i, ffn_f32[0], _ffn_cast_items(i + 1))
    y_prompt, y_sample = _ffn_final(x, xm, ffn_g[1], w_cur, row(final_norm), DEPTH - 1, lay)

    return (y_prompt.reshape(batch, seq, D_MODEL), y_sample.reshape(dec_batch, dec_seq, D_MODEL),
            jnp.stack(swk_p), jnp.stack(swv_p), jnp.stack(mk_p), jnp.stack(mv_p), jnp.stack(conv_p),
            jnp.stack(swk_s), jnp.stack(swv_s), jnp.stack(conv_s))
```
